```python
import jax, jax.numpy as jnp
from jax import lax
import numpy as np

D_MODEL = 1024
BATCH = 4
SEQ = 8192
DEPTH = 2

HG_HEADS = 4
HG_DK = 128
HG_DV = 128
HG_W = HG_HEADS * HG_DK
HG_CHUNK = 64
GM_GROUPS = 4
GM_CHUNK = 128
GM_W = 512
GM_DG = GM_W // GM_GROUPS
SB_HEADS = 8
SB_DH = 64
SB_W = SB_HEADS * SB_DH
SB_BLOCK = 128
N_BRANCH = 3
BRANCH_W = 512
D_FF = 3584
N_EXPERTS = 8
TOP_K = 2
MOE_BLOCK = 256
N_DENSE = (DEPTH + 1) // 2
N_MOE = DEPTH // 2
EPS = 1e-6
IN_SIZES = (HG_W, HG_W, HG_W, HG_W, GM_W, GM_W, SB_W, SB_W, SB_W, N_BRANCH * D_MODEL)
IN_COLS = 4 * HG_W + 2 * GM_W + 3 * SB_W + N_BRANCH * D_MODEL

kernel_name = "hybrid_hgrn2_gmlp_stickbreak_moe"


def rms_norm(x, g):
    xf = x.astype(jnp.float32)
    y = xf * lax.rsqrt(jnp.mean(xf * xf, axis=-1, keepdims=True) + EPS)
    return (y * g.astype(jnp.float32)).astype(x.dtype)


def split_cols(p):
    idx, acc = [], 0
    for s in IN_SIZES[:-1]:
        acc += s
        idx.append(acc)
    return jnp.split(p, idx, axis=-1)


def hgrn2_chunked(q, k, v, log_f):
    B, S, H, DK = q.shape
    DV = v.shape[-1]
    C = HG_CHUNK
    n = S // C

    def to_chunks(a):
        return a.reshape(B, n, C, H, a.shape[-1]).transpose(1, 0, 3, 2, 4)

    qs, ks, vs, gs = to_chunks(q), to_chunks(k), to_chunks(v), to_chunks(log_f)
    causal = jnp.tril(jnp.ones((C, C), bool))[:, :, None]

    def step(state, inp):
        qc, kc, vc, gc = inp
        G = jnp.cumsum(gc, axis=2)
        diff = G[:, :, :, None, :] - G[:, :, None, :, :]
        decay = jnp.exp(jnp.where(causal, diff, -jnp.inf))
        scores = jnp.einsum('bhtd,bhtsd,bhsd->bhts', qc, decay, kc)
        o = (jnp.einsum('bhts,bhsv->bhtv', scores, vc)
             + jnp.einsum('bhtd,bhdv->bhtv', qc * jnp.exp(G), state))
        G_last = G[:, :, -1]
        state = (jnp.exp(G_last)[..., None] * state
                 + jnp.einsum('bhsd,bhsv->bhdv', kc * jnp.exp(G_last[:, :, None] - G), vc))
        return state, o

    s0 = jnp.zeros((B, H, DK, DV), jnp.float32)
    _, o = lax.scan(step, s0, (qs, ks, vs, gs))
    return o.transpose(1, 0, 3, 2, 4).reshape(B, S, H, DV)


def chunk_gmlp(u, v, w_s, b_s, g_v):
    B, S, _ = u.shape
    u = jax.nn.gelu(u)
    v = rms_norm(jax.nn.gelu(v), g_v)
    vr = v.reshape(B, S // GM_CHUNK, GM_CHUNK, GM_GROUPS, GM_DG)
    w_masked = w_s * jnp.tril(jnp.ones((GM_CHUNK, GM_CHUNK), w_s.dtype))[None]
    mix = jnp.einsum('gts,bnsgd->bntgd', w_masked, vr) + b_s.T[None, None, :, :, None]
    return u * mix.reshape(B, S, GM_W)


def stick_breaking_attention(q, k, v):
    S = q.shape[1]
    scale = SB_DH ** -0.5
    outs = []
    for blk in range(S // SB_BLOCK):
        t0, t1 = blk * SB_BLOCK, (blk + 1) * SB_BLOCK
        z = jnp.einsum('bthd,bshd->bhts', q[:, t0:t1], k[:, :t1]).astype(jnp.float32) * scale
        causal = jnp.arange(t1)[None, :] < jnp.arange(t0, t1)[:, None]
        log_keep = jnp.where(causal, jax.nn.log_sigmoid(-z), 0.0)
        suffix = lax.cumsum(log_keep, axis=3, reverse=True) - log_keep
        A = jnp.where(causal, jnp.exp(jax.nn.log_sigmoid(z) + suffix), 0.0)
        outs.append(jnp.einsum('bhts,bshd->bthd', A.astype(v.dtype), v[:, :t1]))
    return jnp.concatenate(outs, axis=1)


def swiglu(h, w_gate, w_up, w_down):
    return (jax.nn.silu(h @ w_gate) * (h @ w_up)) @ w_down


def moe_swiglu(h, w_router, b_router, w_gate, w_up, w_down):
    B, S, D = h.shape
    N = B * S
    NK = N * TOP_K
    hf = h.reshape(N, D)
    logits = hf.astype(jnp.float32) @ w_router.astype(jnp.float32) + b_router.astype(jnp.float32)
    top_logit, top_idx = lax.top_k(logits, TOP_K)
    top_w = jax.nn.softmax(top_logit, axis=-1)
    e_flat = top_idx.reshape(-1)
    tok_flat = jnp.repeat(jnp.arange(N, dtype=jnp.int32), TOP_K)
    w_flat = top_w.reshape(-1)
    order = jnp.argsort(e_flat)
    e_s, tok_s, w_s = e_flat[order], tok_flat[order], w_flat[order]
    counts = jnp.bincount(e_flat, length=N_EXPERTS)
    padded = (counts + MOE_BLOCK - 1) // MOE_BLOCK * MOE_BLOCK
    start = jnp.cumsum(counts) - counts
    pend = jnp.cumsum(padded)
    pstart = pend - padded
    slot = pstart[e_s] + jnp.arange(NK) - start[e_s]
    n_blocks = (NK + MOE_BLOCK - 1) // MOE_BLOCK + N_EXPERTS
    n_slots = n_blocks * MOE_BLOCK
    slot_tok = jnp.full((n_slots,), N, jnp.int32).at[slot].set(tok_s)
    block_expert = jnp.minimum(
        jnp.searchsorted(pend, jnp.arange(n_blocks) * MOE_BLOCK, side='right'), N_EXPERTS - 1)
    xs = jnp.concatenate([hf, jnp.zeros((1, D), hf.dtype)], axis=0)[slot_tok]
    xs = xs.reshape(n_blocks, MOE_BLOCK, D)

    def expert_block(args):
        xb, e = args
        return swiglu(xb, w_gate[e], w_up[e], w_down[e])

    ys = lax.map(expert_block, (xs, block_expert)).reshape(n_slots, D)
    y = jnp.zeros((N, D), h.dtype).at[tok_s].add(ys[slot] * w_s[:, None].astype(h.dtype))
    return y.reshape(B, S, D)


def setup_inputs(seed: int = 0) -> dict:
    key = jax.random.key(seed)
    ks = jax.random.split(key, 24)
    f32 = jnp.float32
    nrm = lambda k, shape, scale: jax.random.normal(k, shape, f32) * scale
    gain = lambda k, shape: 1.0 + 0.02 * jax.random.normal(k, shape, f32)
    return {
        "x": nrm(ks[0], (BATCH, SEQ, D_MODEL), 1.0),
        "norm_mix": gain(ks[1], (DEPTH, D_MODEL)),
        "w_in": nrm(ks[2], (DEPTH, D_MODEL, IN_COLS), D_MODEL ** -0.5),
        "hgrn_lower_bound": nrm(ks[3], (DEPTH, HG_W), 1.0),
        "hgrn_out_norm": gain(ks[4], (DEPTH, HG_DV)),
        "gmlp_v_norm": gain(ks[5], (DEPTH, GM_W)),
        "gmlp_w_s": nrm(ks[6], (DEPTH, GM_GROUPS, GM_CHUNK, GM_CHUNK), GM_CHUNK ** -0.5),
        "gmlp_b_s": gain(ks[7], (DEPTH, GM_GROUPS, GM_CHUNK)),
        "sb_q_norm": gain(ks[8], (DEPTH, SB_DH)),
        "sb_k_norm": gain(ks[9], (DEPTH, SB_DH)),
        "w_branch": nrm(ks[10], (DEPTH, N_BRANCH, BRANCH_W, D_MODEL), BRANCH_W ** -0.5),
        "w_out": nrm(ks[11], (DEPTH, D_MODEL, D_MODEL), D_MODEL ** -0.5),
        "norm_ffn": gain(ks[12], (DEPTH, D_MODEL)),
        "ffn_w_gate": nrm(ks[13], (N_DENSE, D_MODEL, D_FF), D_MODEL ** -0.5),
        "ffn_w_up": nrm(ks[14], (N_DENSE, D_MODEL, D_FF), D_MODEL ** -0.5),
        "ffn_w_down": nrm(ks[15], (N_DENSE, D_FF, D_MODEL), D_FF ** -0.5),
        "router_w": nrm(ks[16], (N_MOE, D_MODEL, N_EXPERTS), D_MODEL ** -0.5),
        "router_b": nrm(ks[17], (N_MOE, N_EXPERTS), 0.01),
        "moe_w_gate": nrm(ks[18], (N_MOE, N_EXPERTS, D_MODEL, D_FF), D_MODEL ** -0.5),
        "moe_w_up": nrm(ks[19], (N_MOE, N_EXPERTS, D_MODEL, D_FF), D_MODEL ** -0.5),
        "moe_w_down": nrm(ks[20], (N_MOE, N_EXPERTS, D_FF, D_MODEL), D_FF ** -0.5),
    }


def reference(x, norm_mix, w_in, hgrn_lower_bound, hgrn_out_norm, gmlp_v_norm, gmlp_w_s, gmlp_b_s,
              sb_q_norm, sb_k_norm, w_branch, w_out, norm_ffn, ffn_w_gate, ffn_w_up, ffn_w_down,
              router_w, router_b, moe_w_gate, moe_w_up, moe_w_down):
    B, S, D = x.shape
    dt = x.dtype
    lb_p = jax.nn.softmax(hgrn_lower_bound.astype(jnp.float32), axis=0)
    lower_bounds = jnp.cumsum(lb_p, axis=0) - lb_p[0:1]
    for l in range(DEPTH):
        xn = rms_norm(x, norm_mix[l])
        p = xn @ w_in[l]
        hq, hf, hi, hg, gu, gv, sq, sk, sv, gates = split_cols(p)

        lb = lower_bounds[l]
        f = lb + (1.0 - lb) * jax.nn.sigmoid(hf.astype(jnp.float32))
        to_heads = lambda a, d: a.reshape(B, S, HG_HEADS, d)
        oa = hgrn2_chunked(to_heads(jax.nn.silu(hq.astype(jnp.float32)), HG_DK),
                           to_heads(1.0 - f, HG_DK),
                           to_heads(hi.astype(jnp.float32), HG_DV),
                           to_heads(jnp.log(f), HG_DK))
        oa = rms_norm(oa, hgrn_out_norm[l]).reshape(B, S, HG_HEADS * HG_DV).astype(dt) * jax.nn.silu(hg)

        ob = chunk_gmlp(gu, gv, gmlp_w_s[l], gmlp_b_s[l], gmlp_v_norm[l])

        qh = rms_norm(sq.reshape(B, S, SB_HEADS, SB_DH), sb_q_norm[l])
        kh = rms_norm(sk.reshape(B, S, SB_HEADS, SB_DH), sb_k_norm[l])
        vh = sv.reshape(B, S, SB_HEADS, SB_DH)
        oc = stick_breaking_attention(qh, kh, vh).reshape(B, S, SB_W)

        o_cat = jnp.stack([oa, ob.astype(dt), oc.astype(dt)], axis=2)
        h = jnp.einsum('bsnc,ncd->bsnd', o_cat, w_branch[l])
        g = jax.nn.sigmoid(gates.reshape(B, S, N_BRANCH, D))
        x = x + jnp.sum(g * h, axis=2) @ w_out[l]

        xn2 = rms_norm(x, norm_ffn[l])
        if l % 2 == 0:
            i = l // 2
            x = x + swiglu(xn2, ffn_w_gate[i], ffn_w_up[i], ffn_w_down[i])
        else:
            i = l // 2
            x = x + moe_swiglu(xn2, router_w[i], router_b[i], moe_w_gate[i], moe_w_up[i], moe_w_down[i])
    return x
```

```python
import functools

import jax
import jax.numpy as jnp
from jax import lax
from jax.experimental import pallas as pl
from jax.experimental.pallas import tpu as pltpu

F32 = jnp.float32
BF16 = jnp.bfloat16
EPS = 1e-6

HG_HEADS = 4
HG_DK = 128
HG_W = HG_HEADS * HG_DK
GM_GROUPS = 4
GM_CHUNK = 128
GM_W = 512
SB_HEADS = 8
SB_DH = 64
SB_W = SB_HEADS * SB_DH
N_BRANCH = 3
N_EXPERTS = 8
TOP_K = 2

LANES = 128
VMEM_LIMIT_BYTES = 56 * 1024 * 1024

HG_CHUNK = 64
HG_SUB = 16
SB_BLOCK = 128
FF_CHUNK = 512
MOE_ROWS = 256
EXP_ZERO_BELOW = -105.0


def _cparams(sem):
    return pltpu.CompilerParams(dimension_semantics=sem, vmem_limit_bytes=VMEM_LIMIT_BYTES)


def _sigmoid(x):
    return 1.0 / (1.0 + jnp.exp(-x))


def _rms(x, gain):
    ms = jnp.mean(x * x, axis=-1, keepdims=True)
    return x * lax.rsqrt(ms + EPS) * gain


def _dot(a, b):
    return jnp.dot(a, b, preferred_element_type=F32)


def _dot_nt(a, b):
    return lax.dot_general(a, b, (((1,), (1,)), ((), ())), preferred_element_type=F32)


def _dot_tn(a, b):
    return lax.dot_general(a, b, (((0,), (0,)), ((), ())), preferred_element_type=F32)


def _inproj_kernel(x_ref, g_ref, w_ref, o_ref, xn_ref):
    @pl.when(pl.program_id(1) == 0)
    def _():
        xn_ref[...] = _rms(x_ref[...], g_ref[...]).astype(BF16)

    o_ref[...] = _dot(xn_ref[...], w_ref[...]).astype(o_ref.dtype)


def _inproj(x2d, gain, w, tm, tn):
    n, d = x2d.shape
    cols = w.shape[1]
    return pl.pallas_call(
        _inproj_kernel,
        grid=(n // tm, cols // tn),
        in_specs=[
            pl.BlockSpec((tm, d), lambda i, j: (i, 0)),
            pl.BlockSpec((1, d), lambda i, j: (0, 0)),
            pl.BlockSpec((d, tn), lambda i, j: (0, j)),
        ],
        out_specs=pl.BlockSpec((tm, tn), lambda i, j: (i, j)),
        out_shape=jax.ShapeDtypeStruct((n, cols), BF16),
        scratch_shapes=[pltpu.VMEM((tm, d), BF16)],
        compiler_params=_cparams(("parallel", "arbitrary")),
        name="inproj",
    )(x2d, gain, w)


def _hgrn_kernel(lb_ref, gn_ref, q_ref, f_ref, i_ref, g_ref, o_ref, st_ref, *, layer, rows):
    C, SUB = HG_CHUNK, HG_SUB

    @pl.when(pl.program_id(1) == 0)
    def _():
        st_ref[...] = jnp.zeros_like(st_ref)

    lb_raw = lb_ref[...]
    lb_e = jnp.exp(lb_raw - jnp.max(lb_raw, axis=0, keepdims=True))
    lb_p = lb_e / jnp.sum(lb_e, axis=0, keepdims=True)
    lb = jnp.sum(lb_p[: layer + 1], axis=0, keepdims=True) - lb_p[0:1]

    r_i = lax.broadcasted_iota(jnp.int32, (C, C), 0)
    c_i = lax.broadcasted_iota(jnp.int32, (C, C), 1)
    ltri = (r_i >= c_i).astype(F32)
    s_idx = lax.broadcasted_iota(jnp.int32, (SUB, HG_DK), 0)
    ones_dk = jnp.ones((HG_DK, HG_DK), BF16)
    sel_r = lax.broadcasted_iota(jnp.int32, (SUB, SUB * SUB), 0)
    sel_c = lax.broadcasted_iota(jnp.int32, (SUB, SUB * SUB), 1)
    sel = jnp.logical_and(sel_c >= sel_r * SUB, sel_c < (sel_r + 1) * SUB).astype(BF16)
    gn = gn_ref[...]

    def chunk(c, carry):
        r0 = pl.multiple_of(c * C, C)
        hq = q_ref[pl.ds(r0, C), :].astype(F32)
        hf = f_ref[pl.ds(r0, C), :].astype(F32)
        hv = i_ref[pl.ds(r0, C), :]
        hg = g_ref[pl.ds(r0, C), :].astype(F32)
        f = lb + (1.0 - lb) * _sigmoid(hf)
        g = jnp.log(f)
        k = 1.0 - f
        q = hq * _sigmoid(hq)
        gate = hg * _sigmoid(hg)
        G = jnp.dot(ltri, g, preferred_element_type=F32, precision=lax.Precision.HIGHEST)
        for h in range(HG_HEADS):
            sl = slice(h * HG_DK, (h + 1) * HG_DK)
            Gh, qh, kh, vh = G[:, sl], q[:, sl], k[:, sl], hv[:, sl]
            st = st_ref[h]
            g_last = Gh[C - 1:C, :]
            o_inter = _dot_nt((qh * jnp.exp(Gh)).astype(BF16), st.astype(BF16))
            k_last = (kh * jnp.exp(g_last - Gh)).astype(BF16)
            st_ref[h] = st * jnp.exp(g_last) + _dot_tn(vh, k_last)
            outs = []
            for i in range(C // SUB):
                rs = slice(i * SUB, (i + 1) * SUB)
                Gi, qi, ki, vi = Gh[rs], qh[rs], kh[rs], vh[rs]
                o_i = o_inter[rs]
                if i > 0:
                    ref_g = Gh[i * SUB - 1:i * SUB, :]
                    qt = (qi * jnp.exp(Gi - ref_g)).astype(BF16)
                    kt = (kh[: i * SUB] * jnp.exp(ref_g - Gh[: i * SUB])).astype(BF16)
                    sc = _dot_nt(qt, kt)
                    o_i = o_i + _dot(sc.astype(BF16), vh[: i * SUB])
                pieces = []
                for t in range(SUB):
                    dec = jnp.where(s_idx <= t, jnp.exp(Gi[t:t + 1, :] - Gi), 0.0)
                    pieces.append(dec * (qi[t:t + 1, :] * ki))
                prod = jnp.concatenate(pieces, axis=0).astype(BF16)
                rsum = _dot(prod, ones_dk)
                wv = rsum * jnp.concatenate([vi.astype(F32)] * SUB, axis=0)
                o_i = o_i + _dot(sel, wv.astype(BF16))
                outs.append(o_i)
            o = jnp.concatenate(outs, axis=0)
            o_ref[pl.ds(r0, C), sl] = (_rms(o, gn) * gate[:, sl]).astype(o_ref.dtype)
        return carry

    lax.fori_loop(0, rows // C, chunk, 0)


def _hgrn(p, lb_all, gn, layer, col0, batch, seq, rows):
    n = p.shape[0]
    nblk = seq // rows
    col = lambda cb: pl.BlockSpec((rows, HG_W), lambda b, s, cb=cb: (b * nblk + s, col0 + cb))
    return pl.pallas_call(
        functools.partial(_hgrn_kernel, layer=layer, rows=rows),
        grid=(batch, nblk),
        in_specs=[
            pl.BlockSpec(lb_all.shape, lambda b, s: (0, 0)),
            pl.BlockSpec((1, HG_DK), lambda b, s: (0, 0)),
            col(0), col(1), col(2), col(3),
        ],
        out_specs=pl.BlockSpec((rows, HG_W), lambda b, s: (b * nblk + s, 0)),
        out_shape=jax.ShapeDtypeStruct((n, HG_W), BF16),
        scratch_shapes=[pltpu.VMEM((HG_HEADS, HG_DK, HG_DK), F32)],
        compiler_params=_cparams(("arbitrary", "arbitrary")),
        name="hgrn2",
    )(lb_all, gn, p, p, p, p)


def _gelu(x):
    return 0.5 * x * (1.0 + jnp.tanh(0.7978845608028654 * (x + 0.044715 * (x * x * x))))


def _gmlp_kernel(gv_ref, ws_ref, bs_ref, u_ref, v_ref, o_ref, *, rows):
    dg = GM_W // GM_GROUPS
    r_i = lax.broadcasted_iota(jnp.int32, (GM_CHUNK, GM_CHUNK), 0)
    c_i = lax.broadcasted_iota(jnp.int32, (GM_CHUNK, GM_CHUNK), 1)
    v = _rms(_gelu(v_ref[...].astype(F32)), gv_ref[...]).astype(BF16)
    for g in range(GM_GROUPS):
        wm = jnp.where(r_i >= c_i, ws_ref[g], 0.0).astype(BF16)
        bias = bs_ref[g]
        sl = slice(g * dg, (g + 1) * dg)
        for c in range(rows // GM_CHUNK):
            rs = slice(c * GM_CHUNK, (c + 1) * GM_CHUNK)
            mix = _dot(wm, v[rs, sl]) + bias
            o_ref[rs, sl] = (_gelu(u_ref[rs, sl].astype(F32)) * mix).astype(o_ref.dtype)


def _gmlp(p, gv, ws, bs, col0, rows):
    n = p.shape[0]
    return pl.pallas_call(
        functools.partial(_gmlp_kernel, rows=rows),
        grid=(n // rows,),
        in_specs=[
            pl.BlockSpec((1, GM_W), lambda i: (0, 0)),
            pl.BlockSpec(ws.shape, lambda i: (0, 0, 0)),
            pl.BlockSpec(bs.shape, lambda i: (0, 0, 0)),
            pl.BlockSpec((rows, GM_W), lambda i: (i, col0)),
            pl.BlockSpec((rows, GM_W), lambda i: (i, col0 + 1)),
        ],
        out_specs=pl.BlockSpec((rows, GM_W), lambda i: (i, 0)),
        out_shape=jax.ShapeDtypeStruct((n, GM_W), BF16),
        compiler_params=_cparams(("parallel",)),
        name="gmlp",
    )(gv, ws, bs, p, p)


def _sb_kernel(qg_ref, kg_ref, u2_ref, q_ref, k_ref, v_ref, o_ref, kh_ref, vh_ref, acc_ref, cr_ref):
    T = SB_BLOCK
    j = pl.program_id(1)
    lane = lax.broadcasted_iota(jnp.int32, (T, LANES), 1)
    low = lane < SB_DH

    def head_norm(x, gain):
        cols = []
        for t in range(SB_W // LANES):
            xt = x[:, t * LANES:(t + 1) * LANES]
            x2 = xt * xt
            lo = jnp.sum(jnp.where(low, x2, 0.0), axis=-1, keepdims=True)
            hi = jnp.sum(jnp.where(low, 0.0, x2), axis=-1, keepdims=True)
            ms = jnp.where(low, lo, hi) * (1.0 / SB_DH)
            cols.append(xt * lax.rsqrt(ms + EPS))
        return jnp.concatenate(cols, axis=1) * gain

    qn = head_norm(q_ref[...].astype(F32), qg_ref[...]) * (SB_DH ** -0.5)
    r0 = pl.multiple_of(j * T, T)
    kh_ref[pl.ds(r0, T), :] = head_norm(k_ref[...].astype(F32), kg_ref[...]).astype(BF16)
    vh_ref[pl.ds(r0, T), :] = v_ref[...]

    row = lax.broadcasted_iota(jnp.int32, (T, T), 0)
    colv = lax.broadcasted_iota(jnp.int32, (T, T), 1)
    causal = colv < row
    u2 = u2_ref[...]

    for p in range(SB_W // LANES):
        ps = slice(p * LANES, (p + 1) * LANES)
        qp = qn[:, ps]
        q_heads = (jnp.where(low, qp, 0.0).astype(BF16), jnp.where(low, 0.0, qp).astype(BF16))

        def tile(kb, masked, q_heads=q_heads, ps=ps):
            k0 = pl.multiple_of(kb * T, T)
            kblk = kh_ref[pl.ds(k0, T), ps]
            vblk = vh_ref[pl.ds(k0, T), ps]
            top = None
            for hh in range(2):
                z = _dot_nt(q_heads[hh], kblk)
                e = jnp.exp(-jnp.abs(z))
                lk = -(jnp.maximum(z, 0.0) + jnp.log(1.0 + e))
                if masked:
                    lk = jnp.where(causal, lk, 0.0)
                ls = lk + z
                lk_hi = lk.astype(BF16)
                lk_lo = (lk - lk_hi.astype(F32)).astype(BF16)
                sums = _dot(jnp.concatenate([lk_hi, lk_lo], axis=1), u2)
                suffix, total = sums[:, :T], sums[:, T:]
                if masked:
                    a = jnp.where(causal, jnp.exp(ls + suffix), 0.0)
                    acc_ref[hh] = _dot(a.astype(BF16), vblk)
                    cr = total
                else:
                    cr_old = cr_ref[hh]
                    a = jnp.exp(ls + suffix + cr_old)
                    acc_ref[hh] += _dot(a.astype(BF16), vblk)
                    cr = cr_old + total
                cr_ref[hh] = cr
                m = jnp.max(cr)
                top = m if top is None else jnp.maximum(top, m)
            return (top > EXP_ZERO_BELOW).astype(jnp.int32)

        alive0 = tile(j, True)

        def cond(state):
            kb, alive = state
            return jnp.logical_and(kb >= 0, alive > 0)

        def body(state, tile=tile):
            kb, _ = state
            return kb - 1, tile(kb, False)

        lax.while_loop(cond, body, (j - 1, alive0))
        o_ref[:, ps] = jnp.where(low, acc_ref[0], acc_ref[1]).astype(o_ref.dtype)


def _sb_suffix_matrix():
    t = SB_BLOCK
    r = jnp.arange(2 * t)[:, None] % t
    c = jnp.arange(2 * t)[None, :]
    return jnp.where(c < t, r > c, True).astype(BF16)


def _sb(p, qg, kg, col0, batch, seq):
    n = p.shape[0]
    t = SB_BLOCK
    nblk = seq // t
    col = lambda cb: pl.BlockSpec((t, SB_W), lambda b, s, cb=cb: (b * nblk + s, cb))
    return pl.pallas_call(
        _sb_kernel,
        grid=(batch, nblk),
        in_specs=[
            pl.BlockSpec((1, SB_W), lambda b, s: (0, 0)),
            pl.BlockSpec((1, SB_W), lambda b, s: (0, 0)),
            pl.BlockSpec((2 * t, 2 * t), lambda b, s: (0, 0)),
            col(col0), col(col0 + 1), col(col0 + 2),
        ],
        out_specs=pl.BlockSpec((t, SB_W), lambda b, s: (b * nblk + s, 0)),
        out_shape=jax.ShapeDtypeStruct((n, SB_W), BF16),
        scratch_shapes=[
            pltpu.VMEM((seq, SB_W), BF16),
            pltpu.VMEM((seq, SB_W), BF16),
            pltpu.VMEM((2, t, LANES), F32),
            pltpu.VMEM((2, t, LANES), F32),
        ],
        compiler_params=_cparams(("arbitrary", "arbitrary")),
        name="stickbreak",
    )(qg, kg, _sb_suffix_matrix(), p, p, p)


def _merge_kernel(x_ref, oa_ref, ob_ref, oc_ref, ga_ref, gb_ref, gc_ref, wb_ref, wo_ref, gn_ref,
                  xo_ref, *maybe_xn_ref):
    mix = None
    for b, (o_ref, g_ref) in enumerate(((oa_ref, ga_ref), (ob_ref, gb_ref), (oc_ref, gc_ref))):
        term = _sigmoid(g_ref[...].astype(F32)) * _dot(o_ref[...], wb_ref[b])
        mix = term if mix is None else mix + term
    x_new = x_ref[...] + _dot(mix.astype(BF16), wo_ref[...])
    xo_ref[...] = x_new
    if maybe_xn_ref:
        maybe_xn_ref[0][...] = _rms(x_new, gn_ref[...]).astype(BF16)


def _merge(x2d, oa, ob, oc, p, gate_col0, wb, wo, gn, rows, emit_norm):
    n, d = x2d.shape
    row = lambda w: pl.BlockSpec((rows, w), lambda i: (i, 0))
    gate = lambda b: pl.BlockSpec((rows, d), lambda i, b=b: (i, gate_col0 + b))
    out_shape = [jax.ShapeDtypeStruct((n, d), F32)]
    out_specs = [row(d)]
    if emit_norm:
        out_shape.append(jax.ShapeDtypeStruct((n, d), BF16))
        out_specs.append(row(d))
    return pl.pallas_call(
        _merge_kernel,
        grid=(n // rows,),
        in_specs=[
            row(d), row(HG_W), row(GM_W), row(SB_W), gate(0), gate(1), gate(2),
            pl.BlockSpec(wb.shape, lambda i: (0, 0, 0)),
            pl.BlockSpec(wo.shape, lambda i: (0, 0)),
            pl.BlockSpec((1, d), lambda i: (0, 0)),
        ],
        out_specs=out_specs,
        out_shape=out_shape,
        compiler_params=_cparams(("parallel",)),
        name="merge",
    )(x2d, oa, ob, oc, p, p, p, wb, wo, gn)


def _swiglu_into(acc_ref, xn, wg_ref, wu_ref, wd_ref):
    acc_ref[...] = jnp.zeros_like(acc_ref)

    def step(c, carry):
        g = _dot(xn, wg_ref[c])
        u = _dot(xn, wu_ref[c])
        acc_ref[...] += _dot((g * _sigmoid(g) * u).astype(BF16), wd_ref[c])
        return carry

    lax.fori_loop(0, wg_ref.shape[0], step, 0)


def _ffn_kernel(x_ref, xn_ref, wg_ref, wu_ref, wd_ref, o_ref, acc_ref):
    _swiglu_into(acc_ref, xn_ref[...], wg_ref, wu_ref, wd_ref)
    o_ref[...] = x_ref[...] + acc_ref[...]


def _resident(shape):
    return pl.BlockSpec(shape, lambda *_: (0,) * len(shape), pipeline_mode=pl.Buffered(1))


def _ffn(x2d, xn, wg, wu, wd, rows):
    n, d = x2d.shape
    return pl.pallas_call(
        _ffn_kernel,
        grid=(n // rows,),
        in_specs=[
            pl.BlockSpec((rows, d), lambda i: (i, 0)),
            pl.BlockSpec((rows, d), lambda i: (i, 0)),
            _resident(wg.shape), _resident(wu.shape), _resident(wd.shape),
        ],
        out_specs=pl.BlockSpec((rows, d), lambda i: (i, 0)),
        out_shape=jax.ShapeDtypeStruct((n, d), F32),
        scratch_shapes=[pltpu.VMEM((rows, d), F32)],
        compiler_params=_cparams(("parallel",)),
        name="ffn",
    )(x2d, xn, wg, wu, wd)


def _router_kernel(x_ref, gn_ref, wr_ref, br_ref, sel_ref, w_ref):
    xn = _rms(x_ref[...], gn_ref[...])
    logits = jnp.dot(xn, wr_ref[...], preferred_element_type=F32,
                     precision=lax.Precision.HIGHEST) + br_ref[...]
    col = lax.broadcasted_iota(jnp.int32, logits.shape, 1)
    m1 = jnp.max(logits, axis=-1, keepdims=True)
    i1 = jnp.min(jnp.where(logits == m1, col, N_EXPERTS), axis=-1, keepdims=True)
    rest = jnp.where(col == i1, -jnp.inf, logits)
    m2 = jnp.max(rest, axis=-1, keepdims=True)
    i2 = jnp.min(jnp.where(rest == m2, col, N_EXPERTS), axis=-1, keepdims=True)
    e = jnp.exp(m2 - m1)
    w1 = 1.0 / (1.0 + e)
    w2 = e / (1.0 + e)
    first, second = col == i1, col == i2
    sel_ref[...] = jnp.logical_or(first, second).astype(jnp.int32)
    w_ref[...] = jnp.where(first, w1, jnp.where(second, w2, 0.0))


def _router(x2d, gn, wr, br, rows):
    n, d = x2d.shape
    return pl.pallas_call(
        _router_kernel,
        grid=(n // rows,),
        in_specs=[
            pl.BlockSpec((rows, d), lambda i: (i, 0)),
            pl.BlockSpec((1, d), lambda i: (0, 0)),
            pl.BlockSpec(wr.shape, lambda i: (0, 0)),
            pl.BlockSpec((1, N_EXPERTS), lambda i: (0, 0)),
        ],
        out_specs=[pl.BlockSpec((rows, N_EXPERTS), lambda i: (i, 0))] * 2,
        out_shape=[jax.ShapeDtypeStruct((n, N_EXPERTS), jnp.int32),
                   jax.ShapeDtypeStruct((n, N_EXPERTS), F32)],
        compiler_params=_cparams(("parallel",)),
        name="router",
    )(x2d, gn, wr, br)


def _row_copy(src_ref, src_row, dst_ref, dst_row, sem):
    return pltpu.make_async_copy(src_ref.at[pl.ds(src_row, 1)], dst_ref.at[pl.ds(dst_row, 1)], sem)


def _dispatch_kernel(slot_ref, x_ref, xs_in_ref, xs_ref, sem, *, rows):
    del xs_in_ref

    def start(r, carry):
        for k in range(TOP_K):
            _row_copy(x_ref, r, xs_ref, slot_ref[0, 0, TOP_K * r + k], sem).start()
        return carry

    def wait(r, carry):
        for k in range(TOP_K):
            _row_copy(x_ref, r, xs_ref, slot_ref[0, 0, TOP_K * r + k], sem).wait()
        return carry

    lax.fori_loop(0, rows, start, 0)
    lax.fori_loop(0, rows, wait, 0)


def _dispatch(x2d, tok_slots, n_slots, rows):
    n, d = x2d.shape
    slots3 = tok_slots.reshape(n // rows, 1, rows * TOP_K)
    return pl.pallas_call(
        functools.partial(_dispatch_kernel, rows=rows),
        grid=(n // rows,),
        in_specs=[
            pl.BlockSpec((1, 1, rows * TOP_K), lambda i: (i, 0, 0), memory_space=pltpu.SMEM),
            pl.BlockSpec((rows, d), lambda i: (i, 0)),
            pl.BlockSpec(memory_space=pl.ANY),
        ],
        out_specs=pl.BlockSpec(memory_space=pl.ANY),
        out_shape=jax.ShapeDtypeStruct((n_slots, d), F32),
        scratch_shapes=[pltpu.SemaphoreType.DMA(())],
        input_output_aliases={2: 0},
        compiler_params=_cparams(("arbitrary",)),
        name="moe_dispatch",
    )(slots3, x2d, jnp.zeros((n_slots, d), F32))


def _experts_kernel(be_ref, nv_ref, xs_ref, gn_ref, wg_ref, wu_ref, wd_ref, o_ref, acc_ref):
    del be_ref
    valid = pl.program_id(0) < nv_ref[0]

    @pl.when(valid)
    def _():
        xn = _rms(xs_ref[...], gn_ref[...]).astype(BF16)
        _swiglu_into(acc_ref, xn, wg_ref, wu_ref, wd_ref)
        o_ref[...] = acc_ref[...]

    @pl.when(jnp.logical_not(valid))
    def _():
        o_ref[...] = jnp.zeros_like(o_ref)


def _experts(xs, gn, wg, wu, wd, block_expert, n_valid, rows):
    n_slots, d = xs.shape
    wspec = lambda w: pl.BlockSpec((None,) + w.shape[1:], lambda i, be, nv: (be[i], 0, 0, 0),
                                   pipeline_mode=pl.Buffered(1))
    xrow = pl.BlockSpec((rows, d), lambda i, be, nv: (i, 0))
    return pl.pallas_call(
        _experts_kernel,
        grid_spec=pltpu.PrefetchScalarGridSpec(
            num_scalar_prefetch=2,
            grid=(n_slots // rows,),
            in_specs=[xrow, pl.BlockSpec((1, d), lambda i, be, nv: (0, 0)), wspec(wg), wspec(wu), wspec(wd)],
            out_specs=xrow,
            scratch_shapes=[pltpu.VMEM((rows, d), F32)],
        ),
        out_shape=jax.ShapeDtypeStruct((n_slots, d), F32),
        compiler_params=_cparams(("arbitrary",)),
        name="moe_experts",
    )(block_expert, n_valid, xs, gn, wg, wu, wd)


def _combine_kernel(slot_ref, x_ref, w_ref, ys_ref, o_ref, buf_ref, sem, *, rows):
    def start(r, carry):
        for k in range(TOP_K):
            _row_copy(ys_ref, slot_ref[0, 0, TOP_K * r + k], buf_ref.at[k], r, sem).start()
        return carry

    def wait(r, carry):
        for k in range(TOP_K):
            _row_copy(ys_ref, slot_ref[0, 0, TOP_K * r + k], buf_ref.at[k], r, sem).wait()
        return carry

    lax.fori_loop(0, rows, start, 0)
    lax.fori_loop(0, rows, wait, 0)
    out = x_ref[...]
    for k in range(TOP_K):
        out = out + w_ref[:, k:k + 1] * buf_ref[k]
    o_ref[...] = out


def _combine(x2d, tok_slots, tok_w, ys, rows):
    n, d = x2d.shape
    slots3 = tok_slots.reshape(n // rows, 1, rows * TOP_K)
    return pl.pallas_call(
        functools.partial(_combine_kernel, rows=rows),
        grid=(n // rows,),
        in_specs=[
            pl.BlockSpec((1, 1, rows * TOP_K), lambda i: (i, 0, 0), memory_space=pltpu.SMEM),
            pl.BlockSpec((rows, d), lambda i: (i, 0)),
            pl.BlockSpec((rows, TOP_K), lambda i: (i, 0)),
            pl.BlockSpec(memory_space=pl.ANY),
        ],
        out_specs=pl.BlockSpec((rows, d), lambda i: (i, 0)),
        out_shape=jax.ShapeDtypeStruct((n, d), F32),
        scratch_shapes=[pltpu.VMEM((TOP_K, rows, d), F32), pltpu.SemaphoreType.DMA(())],
        compiler_params=_cparams(("arbitrary",)),
        name="moe_combine",
    )(slots3, x2d, tok_w, ys)


def _moe(x2d, gn, wr, br, wg, wu, wd):
    n, d = x2d.shape
    sel, w = _router(x2d, gn, wr, br, rows=1024)
    pos = jnp.cumsum(sel, axis=0) - sel
    counts = jnp.sum(sel, axis=0)
    padded = (counts + MOE_ROWS - 1) // MOE_ROWS * MOE_ROWS
    pend = jnp.cumsum(padded)
    pstart = pend - padded
    n_blocks = (n * TOP_K) // MOE_ROWS + N_EXPERTS
    n_slots = n_blocks * MOE_ROWS
    slot = jnp.where(sel > 0, pstart[None, :] + pos, n_slots)
    s_lo = jnp.min(slot, axis=1, keepdims=True)
    s_hi = jnp.min(jnp.where(slot == s_lo, n_slots, slot), axis=1, keepdims=True)
    w_lo = jnp.sum(jnp.where(slot == s_lo, w, 0.0), axis=1, keepdims=True)
    w_hi = jnp.sum(jnp.where(slot == s_hi, w, 0.0), axis=1, keepdims=True)
    tok_slots = jnp.concatenate([s_lo, s_hi], axis=1).astype(jnp.int32)
    tok_w = jnp.concatenate([w_lo, w_hi], axis=1)
    block_expert = jnp.minimum(
        jnp.searchsorted(pend, jnp.arange(n_blocks) * MOE_ROWS, side="right"), N_EXPERTS - 1).astype(jnp.int32)
    n_valid = (pend[-1:] // MOE_ROWS).astype(jnp.int32)

    xs = _dispatch(x2d, tok_slots, n_slots, rows=512)
    ys = _experts(xs, gn, wg, wu, wd, block_expert, n_valid, MOE_ROWS)
    return _combine(x2d, tok_slots, tok_w, ys, rows=256)


def _ff_in(w):
    *lead, d, f = w.shape
    w = w.astype(BF16).reshape(*lead, d, f // FF_CHUNK, FF_CHUNK)
    return jnp.swapaxes(w, -3, -2)


def _ff_out(w):
    *lead, f, d = w.shape
    return w.astype(BF16).reshape(*lead, f // FF_CHUNK, FF_CHUNK, d)


def kernel(x, norm_mix, w_in, hgrn_lower_bound, hgrn_out_norm, gmlp_v_norm, gmlp_w_s, gmlp_b_s,
           sb_q_norm, sb_k_norm, w_branch, w_out, norm_ffn, ffn_w_gate, ffn_w_up, ffn_w_down,
           router_w, router_b, moe_w_gate, moe_w_up, moe_w_down):
    batch, seq, d = x.shape
    depth = w_in.shape[0]
    n = batch * seq
    x2d = x.reshape(n, d)
    n_gate = N_BRANCH * d
    n_mix = w_in.shape[2] - n_gate
    gate_col0 = 0
    hg_col0 = n_gate // HG_W
    gm_col0 = hg_col0 + 4
    sb_col0 = gm_col0 + 2
    for l in range(depth):
        w_l = jnp.concatenate([w_in[l, :, n_mix:], w_in[l, :, :n_mix]], axis=1).astype(BF16)
        p = _inproj(x2d, norm_mix[l][None], w_l, tm=1024, tn=1536)
        oa = _hgrn(p, hgrn_lower_bound, hgrn_out_norm[l][None], l, hg_col0, batch, seq, rows=512)
        ob = _gmlp(p, gmlp_v_norm[l][None], gmlp_w_s[l], gmlp_b_s[l][:, :, None], gm_col0, rows=512)
        oc = _sb(p, jnp.tile(sb_q_norm[l], SB_HEADS)[None], jnp.tile(sb_k_norm[l], SB_HEADS)[None],
                 sb_col0, batch, seq)
        dense = l % 2 == 0
        outs = _merge(x2d, oa, ob, oc, p, gate_col0, w_branch[l].astype(BF16), w_out[l].astype(BF16),
                      norm_ffn[l][None], rows=256, emit_norm=dense)
        i = l // 2
        if dense:
            x2d = _ffn(outs[0], outs[1], _ff_in(ffn_w_gate[i]), _ff_in(ffn_w_up[i]), _ff_out(ffn_w_down[i]),
                       rows=512)
        else:
            x2d = _moe(outs[0], norm_ffn[l][None], router_w[i], router_b[i][None],
                       _ff_in(moe_w_gate[i]), _ff_in(moe_w_up[i]), _ff_out(moe_w_down[i]))
    return x2d.reshape(batch, seq, d)
```

```python
import functools

import jax
import jax.numpy as jnp
from jax import lax
from jax.experimental import pallas as pl
from jax.experimental.pallas import tpu as pltpu

F32 = jnp.float32
BF16 = jnp.bfloat16
EPS = 1e-6

HG_HEADS = 4
HG_DK = 128
HG_W = HG_HEADS * HG_DK
GM_GROUPS = 4
GM_CHUNK = 128
GM_W = 512
SB_HEADS = 8
SB_DH = 64
SB_W = SB_HEADS * SB_DH
N_BRANCH = 3
N_EXPERTS = 8
TOP_K = 2

LANES = 128
VMEM_LIMIT_BYTES = 56 * 1024 * 1024

HG_CHUNK = 64
HG_SUB = 16
SB_BLOCK = 128
FF_CHUNK = 512
MOE_ROWS = 512
EXP_ZERO_BELOW = -105.0


def _cparams(sem):
    return pltpu.CompilerParams(dimension_semantics=sem, vmem_limit_bytes=VMEM_LIMIT_BYTES)


def _sigmoid(x):
    return 1.0 / (1.0 + jnp.exp(-x))


def _rms(x, gain):
    ms = jnp.mean(x * x, axis=-1, keepdims=True)
    return x * lax.rsqrt(ms + EPS) * gain


def _dot(a, b):
    return jnp.dot(a, b, preferred_element_type=F32)


def _dot_nt(a, b):
    return lax.dot_general(a, b, (((1,), (1,)), ((), ())), preferred_element_type=F32)


def _dot_tn(a, b):
    return lax.dot_general(a, b, (((0,), (0,)), ((), ())), preferred_element_type=F32)


def _inproj_kernel(x_ref, g_ref, w_ref, o_ref, xn_ref):
    @pl.when(pl.program_id(1) == 0)
    def _():
        xn_ref[...] = _rms(x_ref[...], g_ref[...]).astype(BF16)

    o_ref[...] = _dot(xn_ref[...], w_ref[...]).astype(o_ref.dtype)


def _inproj(x2d, gain, w, tm, tn):
    n, d = x2d.shape
    cols = w.shape[1]
    return pl.pallas_call(
        _inproj_kernel,
        grid=(n // tm, cols // tn),
        in_specs=[
            pl.BlockSpec((tm, d), lambda i, j: (i, 0)),
            pl.BlockSpec((1, d), lambda i, j: (0, 0)),
            pl.BlockSpec((d, tn), lambda i, j: (0, j)),
        ],
        out_specs=pl.BlockSpec((tm, tn), lambda i, j: (i, j)),
        out_shape=jax.ShapeDtypeStruct((n, cols), BF16),
        scratch_shapes=[pltpu.VMEM((tm, d), BF16)],
        compiler_params=_cparams(("parallel", "arbitrary")),
        name="inproj",
    )(x2d, gain, w)


def _hgrn_kernel(lb_ref, gn_ref, q_ref, f_ref, i_ref, g_ref, o_ref, st_ref, *, layer, rows):
    C, SUB = HG_CHUNK, HG_SUB

    @pl.when(pl.program_id(1) == 0)
    def _():
        st_ref[...] = jnp.zeros_like(st_ref)

    lb_raw = lb_ref[...]
    lb_e = jnp.exp(lb_raw - jnp.max(lb_raw, axis=0, keepdims=True))
    lb_p = lb_e / jnp.sum(lb_e, axis=0, keepdims=True)
    lb = jnp.sum(lb_p[: layer + 1], axis=0, keepdims=True) - lb_p[0:1]

    r_i = lax.broadcasted_iota(jnp.int32, (C, C), 0)
    c_i = lax.broadcasted_iota(jnp.int32, (C, C), 1)
    ltri = (r_i >= c_i).astype(F32)
    s_idx = lax.broadcasted_iota(jnp.int32, (SUB, HG_DK), 0)
    ones_dk = jnp.ones((HG_DK, HG_DK), BF16)
    sel_r = lax.broadcasted_iota(jnp.int32, (SUB, SUB * SUB), 0)
    sel_c = lax.broadcasted_iota(jnp.int32, (SUB, SUB * SUB), 1)
    sel = jnp.logical_and(sel_c >= sel_r * SUB, sel_c < (sel_r + 1) * SUB).astype(BF16)
    gn = gn_ref[...]

    def chunk(c, carry):
        r0 = pl.multiple_of(c * C, C)
        hq = q_ref[pl.ds(r0, C), :].astype(F32)
        hf = f_ref[pl.ds(r0, C), :].astype(F32)
        hv = i_ref[pl.ds(r0, C), :]
        hg = g_ref[pl.ds(r0, C), :].astype(F32)
        f = lb + (1.0 - lb) * _sigmoid(hf)
        g = jnp.log(f)
        k = 1.0 - f
        q = hq * _sigmoid(hq)
        gate = hg * _sigmoid(hg)
        G = jnp.dot(ltri, g, preferred_element_type=F32, precision=lax.Precision.HIGHEST)
        for h in range(HG_HEADS):
            sl = slice(h * HG_DK, (h + 1) * HG_DK)
            Gh, qh, kh, vh = G[:, sl], q[:, sl], k[:, sl], hv[:, sl]
            st = st_ref[h]
            g_last = Gh[C - 1:C, :]
            o_inter = _dot_nt((qh * jnp.exp(Gh)).astype(BF16), st.astype(BF16))
            k_last = (kh * jnp.exp(g_last - Gh)).astype(BF16)
            st_ref[h] = st * jnp.exp(g_last) + _dot_tn(vh, k_last)
            outs = []
            for i in range(C // SUB):
                rs = slice(i * SUB, (i + 1) * SUB)
                Gi, qi, ki, vi = Gh[rs], qh[rs], kh[rs], vh[rs]
                o_i = o_inter[rs]
                if i > 0:
                    ref_g = Gh[i * SUB - 1:i * SUB, :]
                    qt = (qi * jnp.exp(Gi - ref_g)).astype(BF16)
                    kt = (kh[: i * SUB] * jnp.exp(ref_g - Gh[: i * SUB])).astype(BF16)
                    sc = _dot_nt(qt, kt)
                    o_i = o_i + _dot(sc.astype(BF16), vh[: i * SUB])
                pieces = []
                for t in range(SUB):
                    dec = jnp.where(s_idx <= t, jnp.exp(Gi[t:t + 1, :] - Gi), 0.0)
                    pieces.append(dec * (qi[t:t + 1, :] * ki))
                prod = jnp.concatenate(pieces, axis=0).astype(BF16)
                rsum = _dot(prod, ones_dk)
                wv = rsum * jnp.concatenate([vi.astype(F32)] * SUB, axis=0)
                o_i = o_i + _dot(sel, wv.astype(BF16))
                outs.append(o_i)
            o = jnp.concatenate(outs, axis=0)
            o_ref[pl.ds(r0, C), sl] = (_rms(o, gn) * gate[:, sl]).astype(o_ref.dtype)
        return carry

    lax.fori_loop(0, rows // C, chunk, 0)


def _hgrn(p, lb_all, gn, layer, col0, batch, seq, rows):
    n = p.shape[0]
    nblk = seq // rows
    col = lambda cb: pl.BlockSpec((rows, HG_W), lambda b, s, cb=cb: (b * nblk + s, col0 + cb))
    return pl.pallas_call(
        functools.partial(_hgrn_kernel, layer=layer, rows=rows),
        grid=(batch, nblk),
        in_specs=[
            pl.BlockSpec(lb_all.shape, lambda b, s: (0, 0)),
            pl.BlockSpec((1, HG_DK), lambda b, s: (0, 0)),
            col(0), col(1), col(2), col(3),
        ],
        out_specs=pl.BlockSpec((rows, HG_W), lambda b, s: (b * nblk + s, 0)),
        out_shape=jax.ShapeDtypeStruct((n, HG_W), BF16),
        scratch_shapes=[pltpu.VMEM((HG_HEADS, HG_DK, HG_DK), F32)],
        compiler_params=_cparams(("arbitrary", "arbitrary")),
        name="hgrn2",
    )(lb_all, gn, p, p, p, p)


def _gelu(x):
    return 0.5 * x * (1.0 + jnp.tanh(0.7978845608028654 * (x + 0.044715 * (x * x * x))))


def _gmlp_kernel(gv_ref, ws_ref, bs_ref, u_ref, v_ref, o_ref, *, rows):
    dg = GM_W // GM_GROUPS
    r_i = lax.broadcasted_iota(jnp.int32, (GM_CHUNK, GM_CHUNK), 0)
    c_i = lax.broadcasted_iota(jnp.int32, (GM_CHUNK, GM_CHUNK), 1)
    v = _rms(_gelu(v_ref[...].astype(F32)), gv_ref[...]).astype(BF16)
    for g in range(GM_GROUPS):
        wm = jnp.where(r_i >= c_i, ws_ref[g], 0.0).astype(BF16)
        bias = bs_ref[g]
        sl = slice(g * dg, (g + 1) * dg)
        for c in range(rows // GM_CHUNK):
            rs = slice(c * GM_CHUNK, (c + 1) * GM_CHUNK)
            mix = _dot(wm, v[rs, sl]) + bias
            o_ref[rs, sl] = (_gelu(u_ref[rs, sl].astype(F32)) * mix).astype(o_ref.dtype)


def _gmlp(p, gv, ws, bs, col0, rows):
    n = p.shape[0]
    return pl.pallas_call(
        functools.partial(_gmlp_kernel, rows=rows),
        grid=(n // rows,),
        in_specs=[
            pl.BlockSpec((1, GM_W), lambda i: (0, 0)),
            pl.BlockSpec(ws.shape, lambda i: (0, 0, 0)),
            pl.BlockSpec(bs.shape, lambda i: (0, 0, 0)),
            pl.BlockSpec((rows, GM_W), lambda i: (i, col0)),
            pl.BlockSpec((rows, GM_W), lambda i: (i, col0 + 1)),
        ],
        out_specs=pl.BlockSpec((rows, GM_W), lambda i: (i, 0)),
        out_shape=jax.ShapeDtypeStruct((n, GM_W), BF16),
        compiler_params=_cparams(("parallel",)),
        name="gmlp",
    )(gv, ws, bs, p, p)


def _sb_kernel(qg_ref, kg_ref, u2_ref, q_ref, k_ref, v_ref, o_ref, kh_ref, vh_ref, acc_ref, cr_ref,
               z_ref, lkc_ref, sums_ref, a_ref):
    T = SB_BLOCK
    j = pl.program_id(1)
    lane = lax.broadcasted_iota(jnp.int32, (T, LANES), 1)
    low = lane < SB_DH

    def head_norm(x, gain):
        cols = []
        for t in range(SB_W // LANES):
            xt = x[:, t * LANES:(t + 1) * LANES]
            x2 = xt * xt
            lo = jnp.sum(jnp.where(low, x2, 0.0), axis=-1, keepdims=True)
            hi = jnp.sum(jnp.where(low, 0.0, x2), axis=-1, keepdims=True)
            ms = jnp.where(low, lo, hi) * (1.0 / SB_DH)
            cols.append(xt * lax.rsqrt(ms + EPS))
        return jnp.concatenate(cols, axis=1) * gain

    qn = head_norm(q_ref[...].astype(F32), qg_ref[...]) * (SB_DH ** -0.5)
    r0 = pl.multiple_of(j * T, T)
    kh_ref[pl.ds(r0, T), :] = head_norm(k_ref[...].astype(F32), kg_ref[...]).astype(BF16)
    vh_ref[pl.ds(r0, T), :] = v_ref[...]

    row = lax.broadcasted_iota(jnp.int32, (T, T), 0)
    colv = lax.broadcasted_iota(jnp.int32, (T, T), 1)
    causal = colv < row
    u2 = u2_ref[...]

    n_pairs = SB_W // LANES
    q_heads = []
    for p in range(n_pairs):
        qp = qn[:, p * LANES:(p + 1) * LANES]
        q_heads.append((jnp.where(low, qp, 0.0).astype(BF16), jnp.where(low, 0.0, qp).astype(BF16)))

    def tile(kb, masked):
        k0 = pl.multiple_of(kb * T, T)
        for h in range(SB_HEADS):
            ps = slice((h // 2) * LANES, (h // 2 + 1) * LANES)
            z_ref[h] = _dot_nt(q_heads[h // 2][h % 2], kh_ref[pl.ds(k0, T), ps])
        for h in range(SB_HEADS):
            z = z_ref[h]
            e = jnp.exp(-jnp.abs(z))
            lk = -(jnp.maximum(z, 0.0) + jnp.log(1.0 + e))
            if masked:
                lk = jnp.where(causal, lk, 0.0)
            z_ref[h] = lk + z
            lk_hi = lk.astype(BF16)
            lkc_ref[h, :, :T] = lk_hi
            lkc_ref[h, :, T:] = (lk - lk_hi.astype(F32)).astype(BF16)
        for h in range(SB_HEADS):
            sums_ref[h] = _dot(lkc_ref[h], u2)
        top = None
        for h in range(SB_HEADS):
            suffix, total = sums_ref[h, :, :T], sums_ref[h, :, T:]
            if masked:
                a_ref[h] = jnp.where(causal, jnp.exp(z_ref[h] + suffix), 0.0).astype(BF16)
                cr = total
            else:
                cr_old = cr_ref[h]
                a_ref[h] = jnp.exp(z_ref[h] + suffix + cr_old).astype(BF16)
                cr = cr_old + total
            cr_ref[h] = cr
            top = cr if top is None else jnp.maximum(top, cr)
        for h in range(SB_HEADS):
            ps = slice((h // 2) * LANES, (h // 2 + 1) * LANES)
            pv = _dot(a_ref[h], vh_ref[pl.ds(k0, T), ps])
            if masked:
                acc_ref[h] = pv
            else:
                acc_ref[h] += pv
        return (jnp.max(top) > EXP_ZERO_BELOW).astype(jnp.int32)

    alive0 = tile(j, True)

    def cond(state):
        kb, alive = state
        return jnp.logical_and(kb >= 0, alive > 0)

    def body(state):
        kb, _ = state
        return kb - 1, tile(kb, False)

    lax.while_loop(cond, body, (j - 1, alive0))
    for p in range(n_pairs):
        o_ref[:, p * LANES:(p + 1) * LANES] = jnp.where(
            low, acc_ref[2 * p], acc_ref[2 * p + 1]).astype(o_ref.dtype)


def _sb_suffix_matrix():
    t = SB_BLOCK
    r = jnp.arange(2 * t)[:, None] % t
    c = jnp.arange(2 * t)[None, :]
    return jnp.where(c < t, r > c, True).astype(BF16)


def _sb(p, qg, kg, col0, batch, seq):
    n = p.shape[0]
    t = SB_BLOCK
    nblk = seq // t
    col = lambda cb: pl.BlockSpec((t, SB_W), lambda b, s, cb=cb: (b * nblk + s, cb))
    return pl.pallas_call(
        _sb_kernel,
        grid=(batch, nblk),
        in_specs=[
            pl.BlockSpec((1, SB_W), lambda b, s: (0, 0)),
            pl.BlockSpec((1, SB_W), lambda b, s: (0, 0)),
            pl.BlockSpec((2 * t, 2 * t), lambda b, s: (0, 0)),
            col(col0), col(col0 + 1), col(col0 + 2),
        ],
        out_specs=pl.BlockSpec((t, SB_W), lambda b, s: (b * nblk + s, 0)),
        out_shape=jax.ShapeDtypeStruct((n, SB_W), BF16),
        scratch_shapes=[
            pltpu.VMEM((seq, SB_W), BF16),
            pltpu.VMEM((seq, SB_W), BF16),
            pltpu.VMEM((SB_HEADS, t, LANES), F32),
            pltpu.VMEM((SB_HEADS, t, LANES), F32),
            pltpu.VMEM((SB_HEADS, t, t), F32),
            pltpu.VMEM((SB_HEADS, t, 2 * t), BF16),
            pltpu.VMEM((SB_HEADS, t, 2 * t), F32),
            pltpu.VMEM((SB_HEADS, t, t), BF16),
        ],
        compiler_params=_cparams(("arbitrary", "arbitrary")),
        name="stickbreak",
    )(qg, kg, _sb_suffix_matrix(), p, p, p)


def _merge_kernel(x_ref, oa_ref, ob_ref, oc_ref, ga_ref, gb_ref, gc_ref, wb_ref, wo_ref, gn_ref,
                  xo_ref, *maybe_xn_ref):
    mix = None
    for b, (o_ref, g_ref) in enumerate(((oa_ref, ga_ref), (ob_ref, gb_ref), (oc_ref, gc_ref))):
        term = _sigmoid(g_ref[...].astype(F32)) * _dot(o_ref[...], wb_ref[b])
        mix = term if mix is None else mix + term
    x_new = x_ref[...] + _dot(mix.astype(BF16), wo_ref[...])
    xo_ref[...] = x_new
    if maybe_xn_ref:
        maybe_xn_ref[0][...] = _rms(x_new, gn_ref[...]).astype(BF16)


def _merge(x2d, oa, ob, oc, p, gate_col0, wb, wo, gn, rows, emit_norm):
    n, d = x2d.shape
    row = lambda w: pl.BlockSpec((rows, w), lambda i: (i, 0))
    gate = lambda b: pl.BlockSpec((rows, d), lambda i, b=b: (i, gate_col0 + b))
    out_shape = [jax.ShapeDtypeStruct((n, d), F32)]
    out_specs = [row(d)]
    if emit_norm:
        out_shape.append(jax.ShapeDtypeStruct((n, d), BF16))
        out_specs.append(row(d))
    return pl.pallas_call(
        _merge_kernel,
        grid=(n // rows,),
        in_specs=[
            row(d), row(HG_W), row(GM_W), row(SB_W), gate(0), gate(1), gate(2),
            pl.BlockSpec(wb.shape, lambda i: (0, 0, 0)),
            pl.BlockSpec(wo.shape, lambda i: (0, 0)),
            pl.BlockSpec((1, d), lambda i: (0, 0)),
        ],
        out_specs=out_specs,
        out_shape=out_shape,
        compiler_params=_cparams(("parallel",)),
        name="merge",
    )(x2d, oa, ob, oc, p, p, p, wb, wo, gn)


def _swiglu_into(acc_ref, xn, wg_ref, wu_ref, wd_ref):
    acc_ref[...] = jnp.zeros_like(acc_ref)

    def step(c, carry):
        g = _dot(xn, wg_ref[c])
        u = _dot(xn, wu_ref[c])
        acc_ref[...] += _dot((g * _sigmoid(g) * u).astype(BF16), wd_ref[c])
        return carry

    lax.fori_loop(0, wg_ref.shape[0], step, 0)


def _ffn_kernel(x_ref, xn_ref, wg_ref, wu_ref, wd_ref, o_ref, acc_ref):
    _swiglu_into(acc_ref, xn_ref[...], wg_ref, wu_ref, wd_ref)
    o_ref[...] = x_ref[...] + acc_ref[...]


def _resident(shape):
    return pl.BlockSpec(shape, lambda *_: (0,) * len(shape), pipeline_mode=pl.Buffered(1))


def _ffn(x2d, xn, wg, wu, wd, rows):
    n, d = x2d.shape
    return pl.pallas_call(
        _ffn_kernel,
        grid=(n // rows,),
        in_specs=[
            pl.BlockSpec((rows, d), lambda i: (i, 0)),
            pl.BlockSpec((rows, d), lambda i: (i, 0)),
            _resident(wg.shape), _resident(wu.shape), _resident(wd.shape),
        ],
        out_specs=pl.BlockSpec((rows, d), lambda i: (i, 0)),
        out_shape=jax.ShapeDtypeStruct((n, d), F32),
        scratch_shapes=[pltpu.VMEM((rows, d), F32)],
        compiler_params=_cparams(("parallel",)),
        name="ffn",
    )(x2d, xn, wg, wu, wd)


def _router_kernel(x_ref, gn_ref, wr_ref, br_ref, sel_ref, w_ref):
    xn = _rms(x_ref[...], gn_ref[...])
    logits = jnp.dot(xn, wr_ref[...], preferred_element_type=F32,
                     precision=lax.Precision.HIGHEST) + br_ref[...]
    col = lax.broadcasted_iota(jnp.int32, logits.shape, 1)
    m1 = jnp.max(logits, axis=-1, keepdims=True)
    i1 = jnp.min(jnp.where(logits == m1, col, N_EXPERTS), axis=-1, keepdims=True)
    rest = jnp.where(col == i1, -jnp.inf, logits)
    m2 = jnp.max(rest, axis=-1, keepdims=True)
    i2 = jnp.min(jnp.where(rest == m2, col, N_EXPERTS), axis=-1, keepdims=True)
    e = jnp.exp(m2 - m1)
    w1 = 1.0 / (1.0 + e)
    w2 = e / (1.0 + e)
    first, second = col == i1, col == i2
    sel_ref[...] = jnp.logical_or(first, second).astype(jnp.int32)
    w_ref[...] = jnp.where(first, w1, jnp.where(second, w2, 0.0))


def _router(x2d, gn, wr, br, rows):
    n, d = x2d.shape
    return pl.pallas_call(
        _router_kernel,
        grid=(n // rows,),
        in_specs=[
            pl.BlockSpec((rows, d), lambda i: (i, 0)),
            pl.BlockSpec((1, d), lambda i: (0, 0)),
            pl.BlockSpec(wr.shape, lambda i: (0, 0)),
            pl.BlockSpec((1, N_EXPERTS), lambda i: (0, 0)),
        ],
        out_specs=[pl.BlockSpec((rows, N_EXPERTS), lambda i: (i, 0))] * 2,
        out_shape=[jax.ShapeDtypeStruct((n, N_EXPERTS), jnp.int32),
                   jax.ShapeDtypeStruct((n, N_EXPERTS), F32)],
        compiler_params=_cparams(("parallel",)),
        name="router",
    )(x2d, gn, wr, br)


def _row_copy(src_ref, src_row, dst_ref, dst_row, sem):
    return pltpu.make_async_copy(src_ref.at[pl.ds(src_row, 1)], dst_ref.at[pl.ds(dst_row, 1)], sem)


def _dispatch_kernel(slot_ref, x_ref, xs_in_ref, xs_ref, sem, *, rows):
    del xs_in_ref

    def start(r, carry):
        for k in range(TOP_K):
            _row_copy(x_ref, r, xs_ref, slot_ref[0, 0, TOP_K * r + k], sem).start(priority=k)
        return carry

    def wait(r, carry):
        for k in range(TOP_K):
            _row_copy(x_ref, r, xs_ref, slot_ref[0, 0, TOP_K * r + k], sem).wait()
        return carry

    lax.fori_loop(0, rows, start, 0)
    lax.fori_loop(0, rows, wait, 0)


def _dispatch(x2d, tok_slots, n_slots, rows):
    n, d = x2d.shape
    slots3 = tok_slots.reshape(n // rows, 1, rows * TOP_K)
    return pl.pallas_call(
        functools.partial(_dispatch_kernel, rows=rows),
        grid=(n // rows,),
        in_specs=[
            pl.BlockSpec((1, 1, rows * TOP_K), lambda i: (i, 0, 0), memory_space=pltpu.SMEM),
            pl.BlockSpec((rows, d), lambda i: (i, 0)),
            pl.BlockSpec(memory_space=pl.ANY),
        ],
        out_specs=pl.BlockSpec(memory_space=pl.ANY),
        out_shape=jax.ShapeDtypeStruct((n_slots, d), F32),
        scratch_shapes=[pltpu.SemaphoreType.DMA(())],
        input_output_aliases={2: 0},
        compiler_params=_cparams(("arbitrary",)),
        name="moe_dispatch",
    )(slots3, x2d, jnp.zeros((n_slots, d), F32))


def _experts_kernel(be_ref, nv_ref, xs_ref, gn_ref, wg_ref, wu_ref, wd_ref, o_ref, acc_ref):
    del be_ref
    valid = pl.program_id(0) < nv_ref[0]

    @pl.when(valid)
    def _():
        xn = _rms(xs_ref[...], gn_ref[...]).astype(BF16)
        _swiglu_into(acc_ref, xn, wg_ref, wu_ref, wd_ref)
        o_ref[...] = acc_ref[...]

    @pl.when(jnp.logical_not(valid))
    def _():
        o_ref[...] = jnp.zeros_like(o_ref)


def _experts(xs, gn, wg, wu, wd, block_expert, n_valid, rows):
    n_slots, d = xs.shape
    wspec = lambda w: pl.BlockSpec((None,) + w.shape[1:], lambda i, be, nv: (be[i], 0, 0, 0),
                                   pipeline_mode=pl.Buffered(1))
    xrow = pl.BlockSpec((rows, d), lambda i, be, nv: (i, 0))
    return pl.pallas_call(
        _experts_kernel,
        grid_spec=pltpu.PrefetchScalarGridSpec(
            num_scalar_prefetch=2,
            grid=(n_slots // rows,),
            in_specs=[xrow, pl.BlockSpec((1, d), lambda i, be, nv: (0, 0)), wspec(wg), wspec(wu), wspec(wd)],
            out_specs=xrow,
            scratch_shapes=[pltpu.VMEM((rows, d), F32)],
        ),
        out_shape=jax.ShapeDtypeStruct((n_slots, d), F32),
        compiler_params=_cparams(("arbitrary",)),
        name="moe_experts",
    )(block_expert, n_valid, xs, gn, wg, wu, wd)


def _combine_kernel(slot_ref, x_ref, w_ref, ys_ref, o_ref, buf_ref, sem, *, rows):
    def start(r, carry):
        for k in range(TOP_K):
            _row_copy(ys_ref, slot_ref[0, 0, TOP_K * r + k], buf_ref.at[k], r, sem).start(priority=k)
        return carry

    def wait(r, carry):
        for k in range(TOP_K):
            _row_copy(ys_ref, slot_ref[0, 0, TOP_K * r + k], buf_ref.at[k], r, sem).wait()
        return carry

    lax.fori_loop(0, rows, start, 0)
    lax.fori_loop(0, rows, wait, 0)
    out = x_ref[...]
    for k in range(TOP_K):
        out = out + w_ref[:, k:k + 1] * buf_ref[k]
    o_ref[...] = out


def _combine(x2d, tok_slots, tok_w, ys, rows):
    n, d = x2d.shape
    slots3 = tok_slots.reshape(n // rows, 1, rows * TOP_K)
    return pl.pallas_call(
        functools.partial(_combine_kernel, rows=rows),
        grid=(n // rows,),
        in_specs=[
            pl.BlockSpec((1, 1, rows * TOP_K), lambda i: (i, 0, 0), memory_space=pltpu.SMEM),
            pl.BlockSpec((rows, d), lambda i: (i, 0)),
            pl.BlockSpec((rows, TOP_K), lambda i: (i, 0)),
            pl.BlockSpec(memory_space=pl.ANY),
        ],
        out_specs=pl.BlockSpec((rows, d), lambda i: (i, 0)),
        out_shape=jax.ShapeDtypeStruct((n, d), F32),
        scratch_shapes=[pltpu.VMEM((TOP_K, rows, d), F32), pltpu.SemaphoreType.DMA(())],
        compiler_params=_cparams(("arbitrary",)),
        name="moe_combine",
    )(slots3, x2d, tok_w, ys)


def _moe(x2d, gn, wr, br, wg, wu, wd):
    n, d = x2d.shape
    sel, w = _router(x2d, gn, wr, br, rows=1024)
    pos = jnp.cumsum(sel, axis=0) - sel
    counts = jnp.sum(sel, axis=0)
    padded = (counts + MOE_ROWS - 1) // MOE_ROWS * MOE_ROWS
    pend = jnp.cumsum(padded)
    pstart = pend - padded
    n_blocks = (n * TOP_K) // MOE_ROWS + N_EXPERTS
    n_slots = n_blocks * MOE_ROWS
    slot = jnp.where(sel > 0, pstart[None, :] + pos, n_slots)
    s_lo = jnp.min(slot, axis=1, keepdims=True)
    s_hi = jnp.min(jnp.where(slot == s_lo, n_slots, slot), axis=1, keepdims=True)
    w_lo = jnp.sum(jnp.where(slot == s_lo, w, 0.0), axis=1, keepdims=True)
    w_hi = jnp.sum(jnp.where(slot == s_hi, w, 0.0), axis=1, keepdims=True)
    tok_slots = jnp.concatenate([s_lo, s_hi], axis=1).astype(jnp.int32)
    tok_w = jnp.concatenate([w_lo, w_hi], axis=1)
    block_expert = jnp.minimum(
        jnp.searchsorted(pend, jnp.arange(n_blocks) * MOE_ROWS, side="right"), N_EXPERTS - 1).astype(jnp.int32)
    n_valid = (pend[-1:] // MOE_ROWS).astype(jnp.int32)

    xs = _dispatch(x2d, tok_slots, n_slots, rows=512)
    ys = _experts(xs, gn, wg, wu, wd, block_expert, n_valid, MOE_ROWS)
    return _combine(x2d, tok_slots, tok_w, ys, rows=256)


def _ff_in(w):
    *lead, d, f = w.shape
    w = w.astype(BF16).reshape(*lead, d, f // FF_CHUNK, FF_CHUNK)
    return jnp.swapaxes(w, -3, -2)


def _ff_out(w):
    *lead, f, d = w.shape
    return w.astype(BF16).reshape(*lead, f // FF_CHUNK, FF_CHUNK, d)


def kernel(x, norm_mix, w_in, hgrn_lower_bound, hgrn_out_norm, gmlp_v_norm, gmlp_w_s, gmlp_b_s,
           sb_q_norm, sb_k_norm, w_branch, w_out, norm_ffn, ffn_w_gate, ffn_w_up, ffn_w_down,
           router_w, router_b, moe_w_gate, moe_w_up, moe_w_down):
    batch, seq, d = x.shape
    depth = w_in.shape[0]
    n = batch * seq
    x2d = x.reshape(n, d)
    n_gate = N_BRANCH * d
    n_mix = w_in.shape[2] - n_gate
    gate_col0 = 0
    hg_col0 = n_gate // HG_W
    gm_col0 = hg_col0 + 4
    sb_col0 = gm_col0 + 2
    for l in range(depth):
        w_l = jnp.concatenate([w_in[l, :, n_mix:], w_in[l, :, :n_mix]], axis=1).astype(BF16)
        p = _inproj(x2d, norm_mix[l][None], w_l, tm=1024, tn=1536)
        oa = _hgrn(p, hgrn_lower_bound, hgrn_out_norm[l][None], l, hg_col0, batch, seq, rows=512)
        ob = _gmlp(p, gmlp_v_norm[l][None], gmlp_w_s[l], gmlp_b_s[l][:, :, None], gm_col0, rows=512)
        oc = _sb(p, jnp.tile(sb_q_norm[l], SB_HEADS)[None], jnp.tile(sb_k_norm[l], SB_HEADS)[None],
                 sb_col0, batch, seq)
        dense = l % 2 == 0
        outs = _merge(x2d, oa, ob, oc, p, gate_col0, w_branch[l].astype(BF16), w_out[l].astype(BF16),
                      norm_ffn[l][None], rows=256, emit_norm=dense)
        i = l // 2
        if dense:
            x2d = _ffn(outs[0], outs[1], _ff_in(ffn_w_gate[i]), _ff_in(ffn_w_up[i]), _ff_out(ffn_w_down[i]),
                       rows=512)
        else:
            x2d = _moe(outs[0], norm_ffn[l][None], router_w[i], router_b[i][None],
                       _ff_in(moe_w_gate[i]), _ff_in(moe_w_up[i]), _ff_out(moe_w_down[i]))
    return x2d.reshape(batch, seq, d)
```

```python
import functools

import jax
import jax.numpy as jnp
from jax import lax
from jax.experimental import pallas as pl
from jax.experimental.pallas import tpu as pltpu

F32 = jnp.float32
BF16 = jnp.bfloat16
EPS = 1e-6

HG_HEADS = 4
HG_DK = 128
HG_W = HG_HEADS * HG_DK
GM_GROUPS = 4
GM_CHUNK = 128
GM_W = 512
SB_HEADS = 8
SB_DH = 64
SB_W = SB_HEADS * SB_DH
N_BRANCH = 3
N_EXPERTS = 8
TOP_K = 2

LANES = 128
VMEM_LIMIT_BYTES = 56 * 1024 * 1024

HG_CHUNK = 64
HG_SUB = 16
SB_BLOCK = 128
FF_CHUNK = 512
MOE_ROWS = 512
EXP_ZERO_BELOW = -105.0


def _cparams(sem):
    return pltpu.CompilerParams(dimension_semantics=sem, vmem_limit_bytes=VMEM_LIMIT_BYTES)


def _sigmoid(x):
    return 1.0 / (1.0 + jnp.exp(-x))


def _rms(x, gain):
    ms = jnp.mean(x * x, axis=-1, keepdims=True)
    return x * lax.rsqrt(ms + EPS) * gain


def _dot(a, b):
    return jnp.dot(a, b, preferred_element_type=F32)


def _dot_nt(a, b):
    return lax.dot_general(a, b, (((1,), (1,)), ((), ())), preferred_element_type=F32)


def _dot_tn(a, b):
    return lax.dot_general(a, b, (((0,), (0,)), ((), ())), preferred_element_type=F32)


def _inproj_kernel(x_ref, g_ref, w_ref, o_ref, xn_ref):
    @pl.when(pl.program_id(1) == 0)
    def _():
        xn_ref[...] = _rms(x_ref[...], g_ref[...]).astype(BF16)

    o_ref[...] = _dot(xn_ref[...], w_ref[...]).astype(o_ref.dtype)


def _inproj(x2d, gain, w, tm, tn):
    n, d = x2d.shape
    cols = w.shape[1]
    return pl.pallas_call(
        _inproj_kernel,
        grid=(n // tm, cols // tn),
        in_specs=[
            pl.BlockSpec((tm, d), lambda i, j: (i, 0)),
            pl.BlockSpec((1, d), lambda i, j: (0, 0)),
            pl.BlockSpec((d, tn), lambda i, j: (0, j)),
        ],
        out_specs=pl.BlockSpec((tm, tn), lambda i, j: (i, j)),
        out_shape=jax.ShapeDtypeStruct((n, cols), BF16),
        scratch_shapes=[pltpu.VMEM((tm, d), BF16)],
        compiler_params=_cparams(("parallel", "arbitrary")),
        name="inproj",
    )(x2d, gain, w)


def _hgrn_kernel(lb_ref, gn_ref, q_ref, f_ref, i_ref, g_ref, o_ref, st_ref, prod_ref, rsum_ref, wv_ref,
                 osum_ref, sc_ref, *, layer, rows):
    C, SUB = HG_CHUNK, HG_SUB

    @pl.when(pl.program_id(1) == 0)
    def _():
        st_ref[...] = jnp.zeros_like(st_ref)

    lb_raw = lb_ref[...]
    lb_e = jnp.exp(lb_raw - jnp.max(lb_raw, axis=0, keepdims=True))
    lb_p = lb_e / jnp.sum(lb_e, axis=0, keepdims=True)
    lb = jnp.sum(lb_p[: layer + 1], axis=0, keepdims=True) - lb_p[0:1]

    r_i = lax.broadcasted_iota(jnp.int32, (C, C), 0)
    c_i = lax.broadcasted_iota(jnp.int32, (C, C), 1)
    ltri = (r_i >= c_i).astype(F32)
    s_idx = lax.broadcasted_iota(jnp.int32, (SUB, HG_DK), 0)
    ones_dk = jnp.ones((HG_DK, HG_DK), BF16)
    sel_r = lax.broadcasted_iota(jnp.int32, (SUB, SUB * SUB), 0)
    sel_c = lax.broadcasted_iota(jnp.int32, (SUB, SUB * SUB), 1)
    sel = jnp.logical_and(sel_c >= sel_r * SUB, sel_c < (sel_r + 1) * SUB).astype(BF16)
    gn = gn_ref[...]

    def chunk(c, carry):
        r0 = pl.multiple_of(c * C, C)
        hq = q_ref[pl.ds(r0, C), :].astype(F32)
        hf = f_ref[pl.ds(r0, C), :].astype(F32)
        hv = i_ref[pl.ds(r0, C), :]
        hg = g_ref[pl.ds(r0, C), :].astype(F32)
        f = lb + (1.0 - lb) * _sigmoid(hf)
        g = jnp.log(f)
        k = 1.0 - f
        q = hq * _sigmoid(hq)
        gate = hg * _sigmoid(hg)
        G = jnp.dot(ltri, g, preferred_element_type=F32, precision=lax.Precision.HIGHEST)
        n_sub = C // SUB
        for h in range(HG_HEADS):
            sl = slice(h * HG_DK, (h + 1) * HG_DK)
            Gh, qh, kh = G[:, sl], q[:, sl], k[:, sl]
            for i in range(n_sub):
                rs = slice(i * SUB, (i + 1) * SUB)
                Gi, qi, ki = Gh[rs], qh[rs], kh[rs]
                pieces = []
                for t in range(SUB):
                    dec = jnp.where(s_idx <= t, jnp.exp(Gi[t:t + 1, :] - Gi), 0.0)
                    pieces.append(dec * (qi[t:t + 1, :] * ki))
                prod_ref[h * n_sub + i] = jnp.concatenate(pieces, axis=0).astype(BF16)
        for h in range(HG_HEADS):
            sl = slice(h * HG_DK, (h + 1) * HG_DK)
            Gh, qh, kh, vh = G[:, sl], q[:, sl], k[:, sl], hv[:, sl]
            st = st_ref[h]
            g_last = Gh[C - 1:C, :]
            osum_ref[h] = _dot_nt((qh * jnp.exp(Gh)).astype(BF16), st.astype(BF16))
            k_last = (kh * jnp.exp(g_last - Gh)).astype(BF16)
            st_ref[h] = st * jnp.exp(g_last) + _dot_tn(vh, k_last)
        for u in range(HG_HEADS * n_sub):
            rsum_ref[u] = _dot(prod_ref[u], ones_dk)
        for h in range(HG_HEADS):
            sl = slice(h * HG_DK, (h + 1) * HG_DK)
            Gh, qh, kh, vh = G[:, sl], q[:, sl], k[:, sl], hv[:, sl]
            for i in range(1, n_sub):
                rs = slice(i * SUB, (i + 1) * SUB)
                ref_g = Gh[i * SUB - 1:i * SUB, :]
                qt = (qh[rs] * jnp.exp(Gh[rs] - ref_g)).astype(BF16)
                kt = (kh[: i * SUB] * jnp.exp(ref_g - Gh[: i * SUB])).astype(BF16)
                sc_ref[h * n_sub + i, :, : i * SUB] = _dot_nt(qt, kt).astype(BF16)
        for h in range(HG_HEADS):
            vh = hv[:, h * HG_DK:(h + 1) * HG_DK]
            for i in range(n_sub):
                vi = vh[i * SUB:(i + 1) * SUB].astype(F32)
                u = h * n_sub + i
                wv_ref[u] = (rsum_ref[u] * jnp.concatenate([vi] * SUB, axis=0)).astype(BF16)
        for h in range(HG_HEADS):
            sl = slice(h * HG_DK, (h + 1) * HG_DK)
            vh = hv[:, sl]
            outs = []
            for i in range(n_sub):
                u = h * n_sub + i
                o_i = osum_ref[h, i * SUB:(i + 1) * SUB, :] + _dot(sel, wv_ref[u])
                if i > 0:
                    o_i = o_i + _dot(sc_ref[u, :, : i * SUB], vh[: i * SUB])
                outs.append(o_i)
            o = jnp.concatenate(outs, axis=0)
            o_ref[pl.ds(r0, C), sl] = (_rms(o, gn) * gate[:, sl]).astype(o_ref.dtype)
        return carry

    lax.fori_loop(0, rows // C, chunk, 0)


def _hgrn(p, lb_all, gn, layer, col0, batch, seq, rows):
    n = p.shape[0]
    nblk = seq // rows
    col = lambda cb: pl.BlockSpec((rows, HG_W), lambda b, s, cb=cb: (b * nblk + s, col0 + cb))
    return pl.pallas_call(
        functools.partial(_hgrn_kernel, layer=layer, rows=rows),
        grid=(batch, nblk),
        in_specs=[
            pl.BlockSpec(lb_all.shape, lambda b, s: (0, 0)),
            pl.BlockSpec((1, HG_DK), lambda b, s: (0, 0)),
            col(0), col(1), col(2), col(3),
        ],
        out_specs=pl.BlockSpec((rows, HG_W), lambda b, s: (b * nblk + s, 0)),
        out_shape=jax.ShapeDtypeStruct((n, HG_W), BF16),
        scratch_shapes=[
            pltpu.VMEM((HG_HEADS, HG_DK, HG_DK), F32),
            pltpu.VMEM((HG_HEADS * (HG_CHUNK // HG_SUB), HG_SUB * HG_SUB, HG_DK), BF16),
            pltpu.VMEM((HG_HEADS * (HG_CHUNK // HG_SUB), HG_SUB * HG_SUB, HG_DK), F32),
            pltpu.VMEM((HG_HEADS * (HG_CHUNK // HG_SUB), HG_SUB * HG_SUB, HG_DK), BF16),
            pltpu.VMEM((HG_HEADS, HG_CHUNK, HG_DK), F32),
            pltpu.VMEM((HG_HEADS * (HG_CHUNK // HG_SUB), HG_SUB, HG_CHUNK), BF16),
        ],
        compiler_params=_cparams(("arbitrary", "arbitrary")),
        name="hgrn2",
    )(lb_all, gn, p, p, p, p)


def _gelu(x):
    return 0.5 * x * (1.0 + jnp.tanh(0.7978845608028654 * (x + 0.044715 * (x * x * x))))


def _gmlp_kernel(gv_ref, ws_ref, bs_ref, u_ref, v_ref, o_ref, *, rows):
    dg = GM_W // GM_GROUPS
    r_i = lax.broadcasted_iota(jnp.int32, (GM_CHUNK, GM_CHUNK), 0)
    c_i = lax.broadcasted_iota(jnp.int32, (GM_CHUNK, GM_CHUNK), 1)
    v = _rms(_gelu(v_ref[...].astype(F32)), gv_ref[...]).astype(BF16)
    for g in range(GM_GROUPS):
        wm = jnp.where(r_i >= c_i, ws_ref[g], 0.0).astype(BF16)
        bias = bs_ref[g]
        sl = slice(g * dg, (g + 1) * dg)
        for c in range(rows // GM_CHUNK):
            rs = slice(c * GM_CHUNK, (c + 1) * GM_CHUNK)
            mix = _dot(wm, v[rs, sl]) + bias
            o_ref[rs, sl] = (_gelu(u_ref[rs, sl].astype(F32)) * mix).astype(o_ref.dtype)


def _gmlp(p, gv, ws, bs, col0, rows):
    n = p.shape[0]
    return pl.pallas_call(
        functools.partial(_gmlp_kernel, rows=rows),
        grid=(n // rows,),
        in_specs=[
            pl.BlockSpec((1, GM_W), lambda i: (0, 0)),
            pl.BlockSpec(ws.shape, lambda i: (0, 0, 0)),
            pl.BlockSpec(bs.shape, lambda i: (0, 0, 0)),
            pl.BlockSpec((rows, GM_W), lambda i: (i, col0)),
            pl.BlockSpec((rows, GM_W), lambda i: (i, col0 + 1)),
        ],
        out_specs=pl.BlockSpec((rows, GM_W), lambda i: (i, 0)),
        out_shape=jax.ShapeDtypeStruct((n, GM_W), BF16),
        compiler_params=_cparams(("parallel",)),
        name="gmlp",
    )(gv, ws, bs, p, p)


def _sb_kernel(qg_ref, kg_ref, u2_ref, q_ref, k_ref, v_ref, o_ref, kh_ref, vh_ref, acc_ref, cr_ref,
               z_ref, lkc_ref, sums_ref, a_ref):
    T = SB_BLOCK
    j = pl.program_id(1)
    lane = lax.broadcasted_iota(jnp.int32, (T, LANES), 1)
    low = lane < SB_DH

    def head_norm(x, gain):
        cols = []
        for t in range(SB_W // LANES):
            xt = x[:, t * LANES:(t + 1) * LANES]
            x2 = xt * xt
            lo = jnp.sum(jnp.where(low, x2, 0.0), axis=-1, keepdims=True)
            hi = jnp.sum(jnp.where(low, 0.0, x2), axis=-1, keepdims=True)
            ms = jnp.where(low, lo, hi) * (1.0 / SB_DH)
            cols.append(xt * lax.rsqrt(ms + EPS))
        return jnp.concatenate(cols, axis=1) * gain

    qn = head_norm(q_ref[...].astype(F32), qg_ref[...]) * (SB_DH ** -0.5)
    r0 = pl.multiple_of(j * T, T)
    kh_ref[pl.ds(r0, T), :] = head_norm(k_ref[...].astype(F32), kg_ref[...]).astype(BF16)
    vh_ref[pl.ds(r0, T), :] = v_ref[...]

    row = lax.broadcasted_iota(jnp.int32, (T, T), 0)
    colv = lax.broadcasted_iota(jnp.int32, (T, T), 1)
    causal = colv < row
    u2 = u2_ref[...]

    n_pairs = SB_W // LANES
    q_heads = []
    for p in range(n_pairs):
        qp = qn[:, p * LANES:(p + 1) * LANES]
        q_heads.append((jnp.where(low, qp, 0.0).astype(BF16), jnp.where(low, 0.0, qp).astype(BF16)))

    def tile(kb, masked):
        k0 = pl.multiple_of(kb * T, T)
        for h in range(SB_HEADS):
            ps = slice((h // 2) * LANES, (h // 2 + 1) * LANES)
            z_ref[h] = _dot_nt(q_heads[h // 2][h % 2], kh_ref[pl.ds(k0, T), ps])
        for h in range(SB_HEADS):
            z = z_ref[h]
            e = jnp.exp(-jnp.abs(z))
            lk = -(jnp.maximum(z, 0.0) + jnp.log(1.0 + e))
            if masked:
                lk = jnp.where(causal, lk, 0.0)
            z_ref[h] = lk + z
            lk_hi = lk.astype(BF16)
            lkc_ref[h, :, :T] = lk_hi
            lkc_ref[h, :, T:] = (lk - lk_hi.astype(F32)).astype(BF16)
        for h in range(SB_HEADS):
            sums_ref[h] = _dot(lkc_ref[h], u2)
        top = None
        for h in range(SB_HEADS):
            suffix, total = sums_ref[h, :, :T], sums_ref[h, :, T:]
            if masked:
                a_ref[h] = jnp.where(causal, jnp.exp(z_ref[h] + suffix), 0.0).astype(BF16)
                cr = total
            else:
                cr_old = cr_ref[h]
                a_ref[h] = jnp.exp(z_ref[h] + suffix + cr_old).astype(BF16)
                cr = cr_old + total
            cr_ref[h] = cr
            top = cr if top is None else jnp.maximum(top, cr)
        for h in range(SB_HEADS):
            ps = slice((h // 2) * LANES, (h // 2 + 1) * LANES)
            pv = _dot(a_ref[h], vh_ref[pl.ds(k0, T), ps])
            if masked:
                acc_ref[h] = pv
            else:
                acc_ref[h] += pv
        return (jnp.max(top) > EXP_ZERO_BELOW).astype(jnp.int32)

    alive0 = tile(j, True)

    def cond(state):
        kb, alive = state
        return jnp.logical_and(kb >= 0, alive > 0)

    def body(state):
        kb, _ = state
        return kb - 1, tile(kb, False)

    lax.while_loop(cond, body, (j - 1, alive0))
    for p in range(n_pairs):
        o_ref[:, p * LANES:(p + 1) * LANES] = jnp.where(
            low, acc_ref[2 * p], acc_ref[2 * p + 1]).astype(o_ref.dtype)


def _sb_suffix_matrix():
    t = SB_BLOCK
    r = jnp.arange(2 * t)[:, None] % t
    c = jnp.arange(2 * t)[None, :]
    return jnp.where(c < t, r > c, True).astype(BF16)


def _sb(p, qg, kg, col0, batch, seq):
    n = p.shape[0]
    t = SB_BLOCK
    nblk = seq // t
    col = lambda cb: pl.BlockSpec((t, SB_W), lambda b, s, cb=cb: (b * nblk + s, cb))
    return pl.pallas_call(
        _sb_kernel,
        grid=(batch, nblk),
        in_specs=[
            pl.BlockSpec((1, SB_W), lambda b, s: (0, 0)),
            pl.BlockSpec((1, SB_W), lambda b, s: (0, 0)),
            pl.BlockSpec((2 * t, 2 * t), lambda b, s: (0, 0)),
            col(col0), col(col0 + 1), col(col0 + 2),
        ],
        out_specs=pl.BlockSpec((t, SB_W), lambda b, s: (b * nblk + s, 0)),
        out_shape=jax.ShapeDtypeStruct((n, SB_W), BF16),
        scratch_shapes=[
            pltpu.VMEM((seq, SB_W), BF16),
            pltpu.VMEM((seq, SB_W), BF16),
            pltpu.VMEM((SB_HEADS, t, LANES), F32),
            pltpu.VMEM((SB_HEADS, t, LANES), F32),
            pltpu.VMEM((SB_HEADS, t, t), F32),
            pltpu.VMEM((SB_HEADS, t, 2 * t), BF16),
            pltpu.VMEM((SB_HEADS, t, 2 * t), F32),
            pltpu.VMEM((SB_HEADS, t, t), BF16),
        ],
        compiler_params=_cparams(("arbitrary", "arbitrary")),
        name="stickbreak",
    )(qg, kg, _sb_suffix_matrix(), p, p, p)


def _merge_kernel(x_ref, oa_ref, ob_ref, oc_ref, ga_ref, gb_ref, gc_ref, wb_ref, wo_ref, gn_ref,
                  xo_ref, *maybe_xn_ref):
    mix = None
    for b, (o_ref, g_ref) in enumerate(((oa_ref, ga_ref), (ob_ref, gb_ref), (oc_ref, gc_ref))):
        term = _sigmoid(g_ref[...].astype(F32)) * _dot(o_ref[...], wb_ref[b])
        mix = term if mix is None else mix + term
    x_new = x_ref[...] + _dot(mix.astype(BF16), wo_ref[...])
    xo_ref[...] = x_new
    if maybe_xn_ref:
        maybe_xn_ref[0][...] = _rms(x_new, gn_ref[...]).astype(BF16)


def _merge(x2d, oa, ob, oc, p, gate_col0, wb, wo, gn, rows, emit_norm):
    n, d = x2d.shape
    row = lambda w: pl.BlockSpec((rows, w), lambda i: (i, 0))
    gate = lambda b: pl.BlockSpec((rows, d), lambda i, b=b: (i, gate_col0 + b))
    out_shape = [jax.ShapeDtypeStruct((n, d), F32)]
    out_specs = [row(d)]
    if emit_norm:
        out_shape.append(jax.ShapeDtypeStruct((n, d), BF16))
        out_specs.append(row(d))
    return pl.pallas_call(
        _merge_kernel,
        grid=(n // rows,),
        in_specs=[
            row(d), row(HG_W), row(GM_W), row(SB_W), gate(0), gate(1), gate(2),
            pl.BlockSpec(wb.shape, lambda i: (0, 0, 0)),
            pl.BlockSpec(wo.shape, lambda i: (0, 0)),
            pl.BlockSpec((1, d), lambda i: (0, 0)),
        ],
        out_specs=out_specs,
        out_shape=out_shape,
        compiler_params=_cparams(("parallel",)),
        name="merge",
    )(x2d, oa, ob, oc, p, p, p, wb, wo, gn)


def _swiglu_into(acc_ref, xn, wg_ref, wu_ref, wd_ref):
    for c in range(wg_ref.shape[1] // FF_CHUNK):
        cs = slice(c * FF_CHUNK, (c + 1) * FF_CHUNK)
        g = _dot(xn, wg_ref[:, cs])
        u = _dot(xn, wu_ref[:, cs])
        part = _dot((g * _sigmoid(g) * u).astype(BF16), wd_ref[cs, :])
        if c == 0:
            acc_ref[...] = part
        else:
            acc_ref[...] += part


def _ffn_kernel(x_ref, xn_ref, wg_ref, wu_ref, wd_ref, o_ref, acc_ref):
    _swiglu_into(acc_ref, xn_ref[...], wg_ref, wu_ref, wd_ref)
    o_ref[...] = x_ref[...] + acc_ref[...]


def _resident(shape):
    return pl.BlockSpec(shape, lambda *_: (0,) * len(shape), pipeline_mode=pl.Buffered(1))


def _ffn(x2d, xn, wg, wu, wd, rows):
    n, d = x2d.shape
    return pl.pallas_call(
        _ffn_kernel,
        grid=(n // rows,),
        in_specs=[
            pl.BlockSpec((rows, d), lambda i: (i, 0)),
            pl.BlockSpec((rows, d), lambda i: (i, 0)),
            _resident(wg.shape), _resident(wu.shape), _resident(wd.shape),
        ],
        out_specs=pl.BlockSpec((rows, d), lambda i: (i, 0)),
        out_shape=jax.ShapeDtypeStruct((n, d), F32),
        scratch_shapes=[pltpu.VMEM((rows, d), F32)],
        compiler_params=_cparams(("parallel",)),
        name="ffn",
    )(x2d, xn, wg, wu, wd)


def _router_kernel(x_ref, gn_ref, wr_ref, br_ref, sel_ref, w_ref):
    xn = _rms(x_ref[...], gn_ref[...])
    logits = jnp.dot(xn, wr_ref[...], preferred_element_type=F32,
                     precision=lax.Precision.HIGHEST) + br_ref[...]
    col = lax.broadcasted_iota(jnp.int32, logits.shape, 1)
    m1 = jnp.max(logits, axis=-1, keepdims=True)
    i1 = jnp.min(jnp.where(logits == m1, col, N_EXPERTS), axis=-1, keepdims=True)
    rest = jnp.where(col == i1, -jnp.inf, logits)
    m2 = jnp.max(rest, axis=-1, keepdims=True)
    i2 = jnp.min(jnp.where(rest == m2, col, N_EXPERTS), axis=-1, keepdims=True)
    e = jnp.exp(m2 - m1)
    w1 = 1.0 / (1.0 + e)
    w2 = e / (1.0 + e)
    first, second = col == i1, col == i2
    sel_ref[...] = jnp.logical_or(first, second).astype(jnp.int32)
    w_ref[...] = jnp.where(first, w1, jnp.where(second, w2, 0.0))


def _router(x2d, gn, wr, br, rows):
    n, d = x2d.shape
    return pl.pallas_call(
        _router_kernel,
        grid=(n // rows,),
        in_specs=[
            pl.BlockSpec((rows, d), lambda i: (i, 0)),
            pl.BlockSpec((1, d), lambda i: (0, 0)),
            pl.BlockSpec(wr.shape, lambda i: (0, 0)),
            pl.BlockSpec((1, N_EXPERTS), lambda i: (0, 0)),
        ],
        out_specs=[pl.BlockSpec((rows, N_EXPERTS), lambda i: (i, 0))] * 2,
        out_shape=[jax.ShapeDtypeStruct((n, N_EXPERTS), jnp.int32),
                   jax.ShapeDtypeStruct((n, N_EXPERTS), F32)],
        compiler_params=_cparams(("parallel",)),
        name="router",
    )(x2d, gn, wr, br)


def _row_copy(src_ref, src_row, dst_ref, dst_row, sem):
    return pltpu.make_async_copy(src_ref.at[pl.ds(src_row, 1)], dst_ref.at[pl.ds(dst_row, 1)], sem)


def _dispatch_kernel(slot_ref, x_ref, xs_in_ref, xs_ref, sem, *, rows):
    del xs_in_ref

    def start(r, carry):
        for k in range(TOP_K):
            _row_copy(x_ref, r, xs_ref, slot_ref[0, 0, TOP_K * r + k], sem).start(priority=k)
        return carry

    def wait(r, carry):
        for k in range(TOP_K):
            _row_copy(x_ref, r, xs_ref, slot_ref[0, 0, TOP_K * r + k], sem).wait()
        return carry

    lax.fori_loop(0, rows, start, 0)
    lax.fori_loop(0, rows, wait, 0)


def _dispatch(x2d, tok_slots, n_slots, rows):
    n, d = x2d.shape
    slots3 = tok_slots.reshape(n // rows, 1, rows * TOP_K)
    return pl.pallas_call(
        functools.partial(_dispatch_kernel, rows=rows),
        grid=(n // rows,),
        in_specs=[
            pl.BlockSpec((1, 1, rows * TOP_K), lambda i: (i, 0, 0), memory_space=pltpu.SMEM),
            pl.BlockSpec((rows, d), lambda i: (i, 0)),
            pl.BlockSpec(memory_space=pl.ANY),
        ],
        out_specs=pl.BlockSpec(memory_space=pl.ANY),
        out_shape=jax.ShapeDtypeStruct((n_slots, d), F32),
        scratch_shapes=[pltpu.SemaphoreType.DMA(())],
        input_output_aliases={2: 0},
        compiler_params=_cparams(("arbitrary",)),
        name="moe_dispatch",
    )(slots3, x2d, jnp.zeros((n_slots, d), F32))


def _experts_kernel(be_ref, nv_ref, xs_ref, gn_ref, wg_ref, wu_ref, wd_ref, o_ref, acc_ref):
    del be_ref
    valid = pl.program_id(0) < nv_ref[0]

    @pl.when(valid)
    def _():
        xn = _rms(xs_ref[...], gn_ref[...]).astype(BF16)
        _swiglu_into(acc_ref, xn, wg_ref, wu_ref, wd_ref)
        o_ref[...] = acc_ref[...]

    @pl.when(jnp.logical_not(valid))
    def _():
        o_ref[...] = jnp.zeros_like(o_ref)


def _experts(xs, gn, wg, wu, wd, block_expert, n_valid, rows):
    n_slots, d = xs.shape
    wspec = lambda w: pl.BlockSpec((None,) + w.shape[1:], lambda i, be, nv: (be[i], 0, 0),
                                   pipeline_mode=pl.Buffered(1))
    xrow = pl.BlockSpec((rows, d), lambda i, be, nv: (i, 0))
    return pl.pallas_call(
        _experts_kernel,
        grid_spec=pltpu.PrefetchScalarGridSpec(
            num_scalar_prefetch=2,
            grid=(n_slots // rows,),
            in_specs=[xrow, pl.BlockSpec((1, d), lambda i, be, nv: (0, 0)), wspec(wg), wspec(wu), wspec(wd)],
            out_specs=xrow,
            scratch_shapes=[pltpu.VMEM((rows, d), F32)],
        ),
        out_shape=jax.ShapeDtypeStruct((n_slots, d), F32),
        compiler_params=_cparams(("arbitrary",)),
        name="moe_experts",
    )(block_expert, n_valid, xs, gn, wg, wu, wd)


def _combine_kernel(slot_ref, x_ref, w_ref, ys_ref, o_ref, buf_ref, sem, *, rows):
    def start(r, carry):
        for k in range(TOP_K):
            _row_copy(ys_ref, slot_ref[0, 0, TOP_K * r + k], buf_ref.at[k], r, sem).start(priority=k)
        return carry

    def wait(r, carry):
        for k in range(TOP_K):
            _row_copy(ys_ref, slot_ref[0, 0, TOP_K * r + k], buf_ref.at[k], r, sem).wait()
        return carry

    lax.fori_loop(0, rows, start, 0)
    lax.fori_loop(0, rows, wait, 0)
    out = x_ref[...]
    for k in range(TOP_K):
        out = out + w_ref[:, k:k + 1] * buf_ref[k]
    o_ref[...] = out


def _combine(x2d, tok_slots, tok_w, ys, rows):
    n, d = x2d.shape
    slots3 = tok_slots.reshape(n // rows, 1, rows * TOP_K)
    return pl.pallas_call(
        functools.partial(_combine_kernel, rows=rows),
        grid=(n // rows,),
        in_specs=[
            pl.BlockSpec((1, 1, rows * TOP_K), lambda i: (i, 0, 0), memory_space=pltpu.SMEM),
            pl.BlockSpec((rows, d), lambda i: (i, 0)),
            pl.BlockSpec((rows, TOP_K), lambda i: (i, 0)),
            pl.BlockSpec(memory_space=pl.ANY),
        ],
        out_specs=pl.BlockSpec((rows, d), lambda i: (i, 0)),
        out_shape=jax.ShapeDtypeStruct((n, d), F32),
        scratch_shapes=[pltpu.VMEM((TOP_K, rows, d), F32), pltpu.SemaphoreType.DMA(())],
        compiler_params=_cparams(("arbitrary",)),
        name="moe_combine",
    )(slots3, x2d, tok_w, ys)


def _moe(x2d, gn, wr, br, wg, wu, wd):
    n, d = x2d.shape
    sel, w = _router(x2d, gn, wr, br, rows=1024)
    pos = jnp.cumsum(sel, axis=0) - sel
    counts = jnp.sum(sel, axis=0)
    padded = (counts + MOE_ROWS - 1) // MOE_ROWS * MOE_ROWS
    pend = jnp.cumsum(padded)
    pstart = pend - padded
    n_blocks = (n * TOP_K) // MOE_ROWS + N_EXPERTS
    n_slots = n_blocks * MOE_ROWS
    slot = jnp.where(sel > 0, pstart[None, :] + pos, n_slots)
    s_lo = jnp.min(slot, axis=1, keepdims=True)
    s_hi = jnp.min(jnp.where(slot == s_lo, n_slots, slot), axis=1, keepdims=True)
    w_lo = jnp.sum(jnp.where(slot == s_lo, w, 0.0), axis=1, keepdims=True)
    w_hi = jnp.sum(jnp.where(slot == s_hi, w, 0.0), axis=1, keepdims=True)
    tok_slots = jnp.concatenate([s_lo, s_hi], axis=1).astype(jnp.int32)
    tok_w = jnp.concatenate([w_lo, w_hi], axis=1)
    block_expert = jnp.minimum(
        jnp.searchsorted(pend, jnp.arange(n_blocks) * MOE_ROWS, side="right"), N_EXPERTS - 1).astype(jnp.int32)
    n_valid = (pend[-1:] // MOE_ROWS).astype(jnp.int32)

    xs = _dispatch(x2d, tok_slots, n_slots, rows=512)
    ys = _experts(xs, gn, wg, wu, wd, block_expert, n_valid, MOE_ROWS)
    return _combine(x2d, tok_slots, tok_w, ys, rows=256)


def _ff_in(w):
    return w.astype(BF16)


def _ff_out(w):
    return w.astype(BF16)


def kernel(x, norm_mix, w_in, hgrn_lower_bound, hgrn_out_norm, gmlp_v_norm, gmlp_w_s, gmlp_b_s,
           sb_q_norm, sb_k_norm, w_branch, w_out, norm_ffn, ffn_w_gate, ffn_w_up, ffn_w_down,
           router_w, router_b, moe_w_gate, moe_w_up, moe_w_down):
    batch, seq, d = x.shape
    depth = w_in.shape[0]
    n = batch * seq
    x2d = x.reshape(n, d)
    n_gate = N_BRANCH * d
    n_mix = w_in.shape[2] - n_gate
    gate_col0 = 0
    hg_col0 = n_gate // HG_W
    gm_col0 = hg_col0 + 4
    sb_col0 = gm_col0 + 2
    for l in range(depth):
        w_l = jnp.concatenate([w_in[l, :, n_mix:], w_in[l, :, :n_mix]], axis=1).astype(BF16)
        p = _inproj(x2d, norm_mix[l][None], w_l, tm=1024, tn=1536)
        oa = _hgrn(p, hgrn_lower_bound, hgrn_out_norm[l][None], l, hg_col0, batch, seq, rows=512)
        ob = _gmlp(p, gmlp_v_norm[l][None], gmlp_w_s[l], gmlp_b_s[l][:, :, None], gm_col0, rows=512)
        oc = _sb(p, jnp.tile(sb_q_norm[l], SB_HEADS)[None], jnp.tile(sb_k_norm[l], SB_HEADS)[None],
                 sb_col0, batch, seq)
        dense = l % 2 == 0
        outs = _merge(x2d, oa, ob, oc, p, gate_col0, w_branch[l].astype(BF16), w_out[l].astype(BF16),
                      norm_ffn[l][None], rows=256, emit_norm=dense)
        i = l // 2
        if dense:
            x2d = _ffn(outs[0], outs[1], _ff_in(ffn_w_gate[i]), _ff_in(ffn_w_up[i]), _ff_out(ffn_w_down[i]),
                       rows=512)
        else:
            x2d = _moe(outs[0], norm_ffn[l][None], router_w[i], router_b[i][None],
                       _ff_in(moe_w_gate[i]), _ff_in(moe_w_up[i]), _ff_out(moe_w_down[i]))
    return x2d.reshape(batch, seq, d)
```

```python
import functools

import jax
import jax.numpy as jnp
from jax import lax
from jax.experimental import pallas as pl
from jax.experimental.pallas import tpu as pltpu

F32 = jnp.float32
BF16 = jnp.bfloat16
EPS = 1e-6

HG_HEADS = 4
HG_DK = 128
HG_W = HG_HEADS * HG_DK
GM_GROUPS = 4
GM_CHUNK = 128
GM_W = 512
SB_HEADS = 8
SB_DH = 64
SB_W = SB_HEADS * SB_DH
N_BRANCH = 3
N_EXPERTS = 8
TOP_K = 2

LANES = 128
VMEM_LIMIT_BYTES = 56 * 1024 * 1024

HG_CHUNK = 64
HG_SUB = 16
SB_BLOCK = 128
FF_CHUNK = 512
MOE_ROWS = 512
EXP_ZERO_BELOW = -105.0


def _cparams(sem):
    return pltpu.CompilerParams(dimension_semantics=sem, vmem_limit_bytes=VMEM_LIMIT_BYTES)


def _sigmoid(x):
    return 1.0 / (1.0 + jnp.exp(-x))


def _rms(x, gain):
    ms = jnp.mean(x * x, axis=-1, keepdims=True)
    return x * lax.rsqrt(ms + EPS) * gain


def _dot(a, b):
    return jnp.dot(a, b, preferred_element_type=F32)


def _dot_nt(a, b):
    return lax.dot_general(a, b, (((1,), (1,)), ((), ())), preferred_element_type=F32)


def _dot_tn(a, b):
    return lax.dot_general(a, b, (((0,), (0,)), ((), ())), preferred_element_type=F32)


def _inproj_kernel(x_ref, g_ref, w_ref, o_ref, xn_ref):
    @pl.when(pl.program_id(1) == 0)
    def _():
        xn_ref[...] = _rms(x_ref[...], g_ref[...]).astype(BF16)

    o_ref[...] = _dot(xn_ref[...], w_ref[...]).astype(o_ref.dtype)


def _inproj(x2d, gain, w, tm, tn):
    n, d = x2d.shape
    cols = w.shape[1]
    return pl.pallas_call(
        _inproj_kernel,
        grid=(n // tm, cols // tn),
        in_specs=[
            pl.BlockSpec((tm, d), lambda i, j: (i, 0)),
            pl.BlockSpec((1, d), lambda i, j: (0, 0)),
            pl.BlockSpec((d, tn), lambda i, j: (0, j)),
        ],
        out_specs=pl.BlockSpec((tm, tn), lambda i, j: (i, j)),
        out_shape=jax.ShapeDtypeStruct((n, cols), BF16),
        scratch_shapes=[pltpu.VMEM((tm, d), BF16)],
        compiler_params=_cparams(("parallel", "arbitrary")),
        name="inproj",
    )(x2d, gain, w)


def _hgrn_kernel(lb_ref, gn_ref, q_ref, f_ref, i_ref, g_ref, o_ref, st_ref, prod_ref, rsum_ref, wv_ref,
                 osum_ref, sc_ref, *, layer, rows):
    C, SUB = HG_CHUNK, HG_SUB

    @pl.when(pl.program_id(1) == 0)
    def _():
        st_ref[...] = jnp.zeros_like(st_ref)

    lb_raw = lb_ref[...]
    lb_e = jnp.exp(lb_raw - jnp.max(lb_raw, axis=0, keepdims=True))
    lb_p = lb_e / jnp.sum(lb_e, axis=0, keepdims=True)
    lb = jnp.sum(lb_p[: layer + 1], axis=0, keepdims=True) - lb_p[0:1]

    r_i = lax.broadcasted_iota(jnp.int32, (C, C), 0)
    c_i = lax.broadcasted_iota(jnp.int32, (C, C), 1)
    ltri = (r_i >= c_i).astype(F32)
    s_idx = lax.broadcasted_iota(jnp.int32, (SUB, HG_DK), 0)
    ones_dk = jnp.ones((HG_DK, HG_DK), BF16)
    sel_r = lax.broadcasted_iota(jnp.int32, (SUB, SUB * SUB), 0)
    sel_c = lax.broadcasted_iota(jnp.int32, (SUB, SUB * SUB), 1)
    sel = jnp.logical_and(sel_c >= sel_r * SUB, sel_c < (sel_r + 1) * SUB).astype(BF16)
    gn = gn_ref[...]

    def chunk(c, carry):
        r0 = pl.multiple_of(c * C, C)
        hq = q_ref[pl.ds(r0, C), :].astype(F32)
        hf = f_ref[pl.ds(r0, C), :].astype(F32)
        hv = i_ref[pl.ds(r0, C), :]
        hg = g_ref[pl.ds(r0, C), :].astype(F32)
        f = lb + (1.0 - lb) * _sigmoid(hf)
        g = jnp.log(f)
        k = 1.0 - f
        q = hq * _sigmoid(hq)
        gate = hg * _sigmoid(hg)
        G = jnp.dot(ltri, g, preferred_element_type=F32, precision=lax.Precision.HIGHEST)
        n_sub = C // SUB
        for h in range(HG_HEADS):
            sl = slice(h * HG_DK, (h + 1) * HG_DK)
            Gh, qh, kh = G[:, sl], q[:, sl], k[:, sl]
            for i in range(n_sub):
                rs = slice(i * SUB, (i + 1) * SUB)
                Gi, qi, ki = Gh[rs], qh[rs], kh[rs]
                pieces = []
                for t in range(SUB):
                    dec = jnp.where(s_idx <= t, jnp.exp(Gi[t:t + 1, :] - Gi), 0.0)
                    pieces.append(dec * (qi[t:t + 1, :] * ki))
                prod_ref[h * n_sub + i] = jnp.concatenate(pieces, axis=0).astype(BF16)
        for h in range(HG_HEADS):
            sl = slice(h * HG_DK, (h + 1) * HG_DK)
            Gh, qh, kh, vh = G[:, sl], q[:, sl], k[:, sl], hv[:, sl]
            st = st_ref[h]
            g_last = Gh[C - 1:C, :]
            osum_ref[h] = _dot_nt((qh * jnp.exp(Gh)).astype(BF16), st.astype(BF16))
            k_last = (kh * jnp.exp(g_last - Gh)).astype(BF16)
            st_ref[h] = st * jnp.exp(g_last) + _dot_tn(vh, k_last)
        for u in range(HG_HEADS * n_sub):
            rsum_ref[u] = _dot(prod_ref[u], ones_dk)
        for h in range(HG_HEADS):
            sl = slice(h * HG_DK, (h + 1) * HG_DK)
            Gh, qh, kh, vh = G[:, sl], q[:, sl], k[:, sl], hv[:, sl]
            for i in range(1, n_sub):
                rs = slice(i * SUB, (i + 1) * SUB)
                ref_g = Gh[i * SUB - 1:i * SUB, :]
                qt = (qh[rs] * jnp.exp(Gh[rs] - ref_g)).astype(BF16)
                kt = (kh[: i * SUB] * jnp.exp(ref_g - Gh[: i * SUB])).astype(BF16)
                sc_ref[h * n_sub + i, :, : i * SUB] = _dot_nt(qt, kt).astype(BF16)
        for h in range(HG_HEADS):
            vh = hv[:, h * HG_DK:(h + 1) * HG_DK]
            for i in range(n_sub):
                vi = vh[i * SUB:(i + 1) * SUB].astype(F32)
                u = h * n_sub + i
                wv_ref[u] = (rsum_ref[u] * jnp.concatenate([vi] * SUB, axis=0)).astype(BF16)
        for h in range(HG_HEADS):
            sl = slice(h * HG_DK, (h + 1) * HG_DK)
            vh = hv[:, sl]
            outs = []
            for i in range(n_sub):
                u = h * n_sub + i
                o_i = osum_ref[h, i * SUB:(i + 1) * SUB, :] + _dot(sel, wv_ref[u])
                if i > 0:
                    o_i = o_i + _dot(sc_ref[u, :, : i * SUB], vh[: i * SUB])
                outs.append(o_i)
            o = jnp.concatenate(outs, axis=0)
            o_ref[pl.ds(r0, C), sl] = (_rms(o, gn) * gate[:, sl]).astype(o_ref.dtype)
        return carry

    lax.fori_loop(0, rows // C, chunk, 0)


def _hgrn(p, lb_all, gn, layer, col0, batch, seq, rows):
    n = p.shape[0]
    nblk = seq // rows
    col = lambda cb: pl.BlockSpec((rows, HG_W), lambda b, s, cb=cb: (b * nblk + s, col0 + cb))
    return pl.pallas_call(
        functools.partial(_hgrn_kernel, layer=layer, rows=rows),
        grid=(batch, nblk),
        in_specs=[
            pl.BlockSpec(lb_all.shape, lambda b, s: (0, 0)),
            pl.BlockSpec((1, HG_DK), lambda b, s: (0, 0)),
            col(0), col(1), col(2), col(3),
        ],
        out_specs=pl.BlockSpec((rows, HG_W), lambda b, s: (b * nblk + s, 0)),
        out_shape=jax.ShapeDtypeStruct((n, HG_W), BF16),
        scratch_shapes=[
            pltpu.VMEM((HG_HEADS, HG_DK, HG_DK), F32),
            pltpu.VMEM((HG_HEADS * (HG_CHUNK // HG_SUB), HG_SUB * HG_SUB, HG_DK), BF16),
            pltpu.VMEM((HG_HEADS * (HG_CHUNK // HG_SUB), HG_SUB * HG_SUB, HG_DK), F32),
            pltpu.VMEM((HG_HEADS * (HG_CHUNK // HG_SUB), HG_SUB * HG_SUB, HG_DK), BF16),
            pltpu.VMEM((HG_HEADS, HG_CHUNK, HG_DK), F32),
            pltpu.VMEM((HG_HEADS * (HG_CHUNK // HG_SUB), HG_SUB, HG_CHUNK), BF16),
        ],
        compiler_params=_cparams(("arbitrary", "arbitrary")),
        name="hgrn2",
    )(lb_all, gn, p, p, p, p)


def _gelu(x):
    return 0.5 * x * (1.0 + jnp.tanh(0.7978845608028654 * (x + 0.044715 * (x * x * x))))


def _gmlp_kernel(gv_ref, ws_ref, bs_ref, u_ref, v_ref, o_ref, *, rows):
    dg = GM_W // GM_GROUPS
    r_i = lax.broadcasted_iota(jnp.int32, (GM_CHUNK, GM_CHUNK), 0)
    c_i = lax.broadcasted_iota(jnp.int32, (GM_CHUNK, GM_CHUNK), 1)
    v = _rms(_gelu(v_ref[...].astype(F32)), gv_ref[...]).astype(BF16)
    for g in range(GM_GROUPS):
        wm = jnp.where(r_i >= c_i, ws_ref[g], 0.0).astype(BF16)
        bias = bs_ref[g]
        sl = slice(g * dg, (g + 1) * dg)
        for c in range(rows // GM_CHUNK):
            rs = slice(c * GM_CHUNK, (c + 1) * GM_CHUNK)
            mix = _dot(wm, v[rs, sl]) + bias
            o_ref[rs, sl] = (_gelu(u_ref[rs, sl].astype(F32)) * mix).astype(o_ref.dtype)


def _gmlp(p, gv, ws, bs, col0, rows):
    n = p.shape[0]
    return pl.pallas_call(
        functools.partial(_gmlp_kernel, rows=rows),
        grid=(n // rows,),
        in_specs=[
            pl.BlockSpec((1, GM_W), lambda i: (0, 0)),
            pl.BlockSpec(ws.shape, lambda i: (0, 0, 0)),
            pl.BlockSpec(bs.shape, lambda i: (0, 0, 0)),
            pl.BlockSpec((rows, GM_W), lambda i: (i, col0)),
            pl.BlockSpec((rows, GM_W), lambda i: (i, col0 + 1)),
        ],
        out_specs=pl.BlockSpec((rows, GM_W), lambda i: (i, 0)),
        out_shape=jax.ShapeDtypeStruct((n, GM_W), BF16),
        compiler_params=_cparams(("parallel",)),
        name="gmlp",
    )(gv, ws, bs, p, p)


def _sb_kernel(qg_ref, kg_ref, u2_ref, q_ref, k_ref, v_ref, o_ref, kh_ref, vh_ref, acc_ref, cr_ref,
               z_ref, lkc_ref, sums_ref, a_ref):
    T = SB_BLOCK
    j = pl.program_id(1)
    lane = lax.broadcasted_iota(jnp.int32, (T, LANES), 1)
    low = lane < SB_DH

    def head_norm(x, gain):
        cols = []
        for t in range(SB_W // LANES):
            xt = x[:, t * LANES:(t + 1) * LANES]
            x2 = xt * xt
            lo = jnp.sum(jnp.where(low, x2, 0.0), axis=-1, keepdims=True)
            hi = jnp.sum(jnp.where(low, 0.0, x2), axis=-1, keepdims=True)
            ms = jnp.where(low, lo, hi) * (1.0 / SB_DH)
            cols.append(xt * lax.rsqrt(ms + EPS))
        return jnp.concatenate(cols, axis=1) * gain

    qn = head_norm(q_ref[...].astype(F32), qg_ref[...]) * (SB_DH ** -0.5)
    r0 = pl.multiple_of(j * T, T)
    kh_ref[pl.ds(r0, T), :] = head_norm(k_ref[...].astype(F32), kg_ref[...]).astype(BF16)
    vh_ref[pl.ds(r0, T), :] = v_ref[...]

    row = lax.broadcasted_iota(jnp.int32, (T, T), 0)
    colv = lax.broadcasted_iota(jnp.int32, (T, T), 1)
    causal = colv < row
    u2 = u2_ref[...]

    n_pairs = SB_W // LANES
    q_heads = []
    for p in range(n_pairs):
        qp = qn[:, p * LANES:(p + 1) * LANES]
        q_heads.append((jnp.where(low, qp, 0.0).astype(BF16), jnp.where(low, 0.0, qp).astype(BF16)))

    def tile(kb, masked):
        k0 = pl.multiple_of(kb * T, T)
        for h in range(SB_HEADS):
            ps = slice((h // 2) * LANES, (h // 2 + 1) * LANES)
            z_ref[h] = _dot_nt(q_heads[h // 2][h % 2], kh_ref[pl.ds(k0, T), ps])
        for h in range(SB_HEADS):
            z = z_ref[h]
            e = jnp.exp(-jnp.abs(z))
            lk = -(jnp.maximum(z, 0.0) + jnp.log(1.0 + e))
            if masked:
                lk = jnp.where(causal, lk, 0.0)
            z_ref[h] = lk + z
            lk_hi = lk.astype(BF16)
            lkc_ref[h, :, :T] = lk_hi
            lkc_ref[h, :, T:] = (lk - lk_hi.astype(F32)).astype(BF16)
        for h in range(SB_HEADS):
            sums_ref[h] = _dot(lkc_ref[h], u2)
        top = None
        for h in range(SB_HEADS):
            suffix, total = sums_ref[h, :, :T], sums_ref[h, :, T:]
            if masked:
                a_ref[h] = jnp.where(causal, jnp.exp(z_ref[h] + suffix), 0.0).astype(BF16)
                cr = total
            else:
                cr_old = cr_ref[h]
                a_ref[h] = jnp.exp(z_ref[h] + suffix + cr_old).astype(BF16)
                cr = cr_old + total
            cr_ref[h] = cr
            top = cr if top is None else jnp.maximum(top, cr)
        for h in range(SB_HEADS):
            ps = slice((h // 2) * LANES, (h // 2 + 1) * LANES)
            pv = _dot(a_ref[h], vh_ref[pl.ds(k0, T), ps])
            if masked:
                acc_ref[h] = pv
            else:
                acc_ref[h] += pv
        return (jnp.max(top) > EXP_ZERO_BELOW).astype(jnp.int32)

    alive0 = tile(j, True)

    def cond(state):
        kb, alive = state
        return jnp.logical_and(kb >= 0, alive > 0)

    def body(state):
        kb, _ = state
        return kb - 1, tile(kb, False)

    lax.while_loop(cond, body, (j - 1, alive0))
    for p in range(n_pairs):
        o_ref[:, p * LANES:(p + 1) * LANES] = jnp.where(
            low, acc_ref[2 * p], acc_ref[2 * p + 1]).astype(o_ref.dtype)


def _sb_suffix_matrix():
    t = SB_BLOCK
    r = jnp.arange(2 * t)[:, None] % t
    c = jnp.arange(2 * t)[None, :]
    return jnp.where(c < t, r > c, True).astype(BF16)


def _sb(p, qg, kg, col0, batch, seq):
    n = p.shape[0]
    t = SB_BLOCK
    nblk = seq // t
    col = lambda cb: pl.BlockSpec((t, SB_W), lambda b, s, cb=cb: (b * nblk + s, cb))
    return pl.pallas_call(
        _sb_kernel,
        grid=(batch, nblk),
        in_specs=[
            pl.BlockSpec((1, SB_W), lambda b, s: (0, 0)),
            pl.BlockSpec((1, SB_W), lambda b, s: (0, 0)),
            pl.BlockSpec((2 * t, 2 * t), lambda b, s: (0, 0)),
            col(col0), col(col0 + 1), col(col0 + 2),
        ],
        out_specs=pl.BlockSpec((t, SB_W), lambda b, s: (b * nblk + s, 0)),
        out_shape=jax.ShapeDtypeStruct((n, SB_W), BF16),
        scratch_shapes=[
            pltpu.VMEM((seq, SB_W), BF16),
            pltpu.VMEM((seq, SB_W), BF16),
            pltpu.VMEM((SB_HEADS, t, LANES), F32),
            pltpu.VMEM((SB_HEADS, t, LANES), F32),
            pltpu.VMEM((SB_HEADS, t, t), F32),
            pltpu.VMEM((SB_HEADS, t, 2 * t), BF16),
            pltpu.VMEM((SB_HEADS, t, 2 * t), F32),
            pltpu.VMEM((SB_HEADS, t, t), BF16),
        ],
        compiler_params=_cparams(("arbitrary", "arbitrary")),
        name="stickbreak",
    )(qg, kg, _sb_suffix_matrix(), p, p, p)


def _merge_kernel(x_ref, oa_ref, ob_ref, oc_ref, ga_ref, gb_ref, gc_ref, wb_ref, wo_ref, gn_ref,
                  xo_ref, *maybe_xn_ref):
    mix = None
    for b, (o_ref, g_ref) in enumerate(((oa_ref, ga_ref), (ob_ref, gb_ref), (oc_ref, gc_ref))):
        term = _sigmoid(g_ref[...].astype(F32)) * _dot(o_ref[...], wb_ref[b])
        mix = term if mix is None else mix + term
    x_new = x_ref[...] + _dot(mix.astype(BF16), wo_ref[...])
    xo_ref[...] = x_new
    if maybe_xn_ref:
        maybe_xn_ref[0][...] = _rms(x_new, gn_ref[...]).astype(BF16)


def _merge(x2d, oa, ob, oc, p, gate_col0, wb, wo, gn, rows, emit_norm):
    n, d = x2d.shape
    row = lambda w: pl.BlockSpec((rows, w), lambda i: (i, 0))
    gate = lambda b: pl.BlockSpec((rows, d), lambda i, b=b: (i, gate_col0 + b))
    out_shape = [jax.ShapeDtypeStruct((n, d), F32)]
    out_specs = [row(d)]
    if emit_norm:
        out_shape.append(jax.ShapeDtypeStruct((n, d), BF16))
        out_specs.append(row(d))
    return pl.pallas_call(
        _merge_kernel,
        grid=(n // rows,),
        in_specs=[
            row(d), row(HG_W), row(GM_W), row(SB_W), gate(0), gate(1), gate(2),
            pl.BlockSpec(wb.shape, lambda i: (0, 0, 0)),
            pl.BlockSpec(wo.shape, lambda i: (0, 0)),
            pl.BlockSpec((1, d), lambda i: (0, 0)),
        ],
        out_specs=out_specs,
        out_shape=out_shape,
        compiler_params=_cparams(("parallel",)),
        name="merge",
    )(x2d, oa, ob, oc, p, p, p, wb, wo, gn)


def _swiglu_into(acc_ref, xn, wg_ref, wu_ref, wd_ref):
    for c in range(wg_ref.shape[1] // FF_CHUNK):
        cs = slice(c * FF_CHUNK, (c + 1) * FF_CHUNK)
        g = _dot(xn, wg_ref[:, cs])
        u = _dot(xn, wu_ref[:, cs])
        part = _dot((g * _sigmoid(g) * u).astype(BF16), wd_ref[cs, :])
        if c == 0:
            acc_ref[...] = part
        else:
            acc_ref[...] += part


def _ffn_kernel(x_ref, xn_ref, wg_ref, wu_ref, wd_ref, o_ref, acc_ref):
    _swiglu_into(acc_ref, xn_ref[...], wg_ref, wu_ref, wd_ref)
    o_ref[...] = x_ref[...] + acc_ref[...]


def _resident(shape):
    return pl.BlockSpec(shape, lambda *_: (0,) * len(shape), pipeline_mode=pl.Buffered(1))


def _ffn(x2d, xn, wg, wu, wd, rows):
    n, d = x2d.shape
    return pl.pallas_call(
        _ffn_kernel,
        grid=(n // rows,),
        in_specs=[
            pl.BlockSpec((rows, d), lambda i: (i, 0)),
            pl.BlockSpec((rows, d), lambda i: (i, 0)),
            _resident(wg.shape), _resident(wu.shape), _resident(wd.shape),
        ],
        out_specs=pl.BlockSpec((rows, d), lambda i: (i, 0)),
        out_shape=jax.ShapeDtypeStruct((n, d), F32),
        scratch_shapes=[pltpu.VMEM((rows, d), F32)],
        compiler_params=_cparams(("parallel",)),
        name="ffn",
    )(x2d, xn, wg, wu, wd)


def _router_kernel(x_ref, gn_ref, wr_ref, br_ref, sel_ref, w_ref):
    xn = _rms(x_ref[...], gn_ref[...])
    logits = jnp.dot(xn, wr_ref[...], preferred_element_type=F32,
                     precision=lax.Precision.HIGHEST) + br_ref[...]
    col = lax.broadcasted_iota(jnp.int32, logits.shape, 1)
    m1 = jnp.max(logits, axis=-1, keepdims=True)
    i1 = jnp.min(jnp.where(logits == m1, col, N_EXPERTS), axis=-1, keepdims=True)
    rest = jnp.where(col == i1, -jnp.inf, logits)
    m2 = jnp.max(rest, axis=-1, keepdims=True)
    i2 = jnp.min(jnp.where(rest == m2, col, N_EXPERTS), axis=-1, keepdims=True)
    e = jnp.exp(m2 - m1)
    w1 = 1.0 / (1.0 + e)
    w2 = e / (1.0 + e)
    first, second = col == i1, col == i2
    sel_ref[...] = jnp.logical_or(first, second).astype(jnp.int32)
    w_ref[...] = jnp.where(first, w1, jnp.where(second, w2, 0.0))


def _router(x2d, gn, wr, br, rows):
    n, d = x2d.shape
    return pl.pallas_call(
        _router_kernel,
        grid=(n // rows,),
        in_specs=[
            pl.BlockSpec((rows, d), lambda i: (i, 0)),
            pl.BlockSpec((1, d), lambda i: (0, 0)),
            pl.BlockSpec(wr.shape, lambda i: (0, 0)),
            pl.BlockSpec((1, N_EXPERTS), lambda i: (0, 0)),
        ],
        out_specs=[pl.BlockSpec((rows, N_EXPERTS), lambda i: (i, 0))] * 2,
        out_shape=[jax.ShapeDtypeStruct((n, N_EXPERTS), jnp.int32),
                   jax.ShapeDtypeStruct((n, N_EXPERTS), F32)],
        compiler_params=_cparams(("parallel",)),
        name="router",
    )(x2d, gn, wr, br)


def _row_copy(src_ref, src_row, dst_ref, dst_row, sem):
    return pltpu.make_async_copy(src_ref.at[pl.ds(src_row, 1)], dst_ref.at[pl.ds(dst_row, 1)], sem)


def _dispatch_kernel(slot_ref, x_ref, xs_in_ref, xs_ref, sem, *, rows):
    del xs_in_ref

    def start(r, carry):
        for k in range(TOP_K):
            _row_copy(x_ref, r, xs_ref, slot_ref[0, 0, TOP_K * r + k], sem).start(priority=k)
        return carry

    def wait(r, carry):
        for k in range(TOP_K):
            _row_copy(x_ref, r, xs_ref, slot_ref[0, 0, TOP_K * r + k], sem).wait()
        return carry

    lax.fori_loop(0, rows, start, 0)
    lax.fori_loop(0, rows, wait, 0)


def _dispatch(x2d, tok_slots, n_slots, rows):
    n, d = x2d.shape
    slots3 = tok_slots.reshape(n // rows, 1, rows * TOP_K)
    return pl.pallas_call(
        functools.partial(_dispatch_kernel, rows=rows),
        grid=(n // rows,),
        in_specs=[
            pl.BlockSpec((1, 1, rows * TOP_K), lambda i: (i, 0, 0), memory_space=pltpu.SMEM),
            pl.BlockSpec((rows, d), lambda i: (i, 0)),
            pl.BlockSpec(memory_space=pl.ANY),
        ],
        out_specs=pl.BlockSpec(memory_space=pl.ANY),
        out_shape=jax.ShapeDtypeStruct((n_slots, d), F32),
        scratch_shapes=[pltpu.SemaphoreType.DMA(())],
        input_output_aliases={2: 0},
        compiler_params=_cparams(("arbitrary",)),
        name="moe_dispatch",
    )(slots3, x2d, jnp.zeros((n_slots, d), F32))


def _experts_kernel(be_ref, nv_ref, xs_ref, gn_ref, wg_ref, wu_ref, wd_ref, o_ref, acc_ref):
    del be_ref
    valid = pl.program_id(0) < nv_ref[0]

    @pl.when(valid)
    def _():
        xn = _rms(xs_ref[...], gn_ref[...]).astype(BF16)
        _swiglu_into(acc_ref, xn, wg_ref, wu_ref, wd_ref)
        o_ref[...] = acc_ref[...]

    @pl.when(jnp.logical_not(valid))
    def _():
        o_ref[...] = jnp.zeros_like(o_ref)


def _experts(xs, gn, wg, wu, wd, block_expert, n_valid, rows):
    n_slots, d = xs.shape
    wspec = lambda w: pl.BlockSpec((None,) + w.shape[1:], lambda i, be, nv: (be[i], 0, 0),
                                   pipeline_mode=pl.Buffered(1))
    xrow = pl.BlockSpec((rows, d), lambda i, be, nv: (i, 0))
    return pl.pallas_call(
        _experts_kernel,
        grid_spec=pltpu.PrefetchScalarGridSpec(
            num_scalar_prefetch=2,
            grid=(n_slots // rows,),
            in_specs=[xrow, pl.BlockSpec((1, d), lambda i, be, nv: (0, 0)), wspec(wg), wspec(wu), wspec(wd)],
            out_specs=xrow,
            scratch_shapes=[pltpu.VMEM((rows, d), F32)],
        ),
        out_shape=jax.ShapeDtypeStruct((n_slots, d), F32),
        compiler_params=_cparams(("arbitrary",)),
        name="moe_experts",
    )(block_expert, n_valid, xs, gn, wg, wu, wd)


def _combine_kernel(slot_ref, x_ref, w_ref, ys_ref, o_ref, buf_ref, sem, *, rows):
    def start(r, carry):
        for k in range(TOP_K):
            _row_copy(ys_ref, slot_ref[0, 0, TOP_K * r + k], buf_ref.at[k], r, sem).start(priority=k)
        return carry

    def wait(r, carry):
        for k in range(TOP_K):
            _row_copy(ys_ref, slot_ref[0, 0, TOP_K * r + k], buf_ref.at[k], r, sem).wait()
        return carry

    lax.fori_loop(0, rows, start, 0)
    lax.fori_loop(0, rows, wait, 0)
    out = x_ref[...]
    for k in range(TOP_K):
        out = out + w_ref[:, k:k + 1] * buf_ref[k]
    o_ref[...] = out


def _combine(x2d, tok_slots, tok_w, ys, rows):
    n, d = x2d.shape
    slots3 = tok_slots.reshape(n // rows, 1, rows * TOP_K)
    return pl.pallas_call(
        functools.partial(_combine_kernel, rows=rows),
        grid=(n // rows,),
        in_specs=[
            pl.BlockSpec((1, 1, rows * TOP_K), lambda i: (i, 0, 0), memory_space=pltpu.SMEM),
            pl.BlockSpec((rows, d), lambda i: (i, 0)),
            pl.BlockSpec((rows, TOP_K), lambda i: (i, 0)),
            pl.BlockSpec(memory_space=pl.ANY),
        ],
        out_specs=pl.BlockSpec((rows, d), lambda i: (i, 0)),
        out_shape=jax.ShapeDtypeStruct((n, d), F32),
        scratch_shapes=[pltpu.VMEM((TOP_K, rows, d), F32), pltpu.SemaphoreType.DMA(())],
        compiler_params=_cparams(("arbitrary",)),
        name="moe_combine",
    )(slots3, x2d, tok_w, ys)


def _moe(x2d, gn, wr, br, wg, wu, wd):
    n, d = x2d.shape
    sel, w = _router(x2d, gn, wr, br, rows=1024)
    pos = jnp.cumsum(sel, axis=0) - sel
    counts = jnp.sum(sel, axis=0)
    padded = (counts + MOE_ROWS - 1) // MOE_ROWS * MOE_ROWS
    pend = jnp.cumsum(padded)
    pstart = pend - padded
    n_blocks = (n * TOP_K) // MOE_ROWS + N_EXPERTS
    n_slots = n_blocks * MOE_ROWS
    slot = jnp.where(sel > 0, pstart[None, :] + pos, n_slots)
    s_lo = jnp.min(slot, axis=1, keepdims=True)
    s_hi = jnp.min(jnp.where(slot == s_lo, n_slots, slot), axis=1, keepdims=True)
    w_lo = jnp.sum(jnp.where(slot == s_lo, w, 0.0), axis=1, keepdims=True)
    w_hi = jnp.sum(jnp.where(slot == s_hi, w, 0.0), axis=1, keepdims=True)
    tok_slots = jnp.concatenate([s_lo, s_hi], axis=1).astype(jnp.int32)
    tok_w = jnp.concatenate([w_lo, w_hi], axis=1)
    block_expert = jnp.minimum(
        jnp.searchsorted(pend, jnp.arange(n_blocks) * MOE_ROWS, side="right"), N_EXPERTS - 1).astype(jnp.int32)
    n_valid = (pend[-1:] // MOE_ROWS).astype(jnp.int32)

    xs = _dispatch(x2d, tok_slots, n_slots, rows=512)
    ys = _experts(xs, gn, wg, wu, wd, block_expert, n_valid, MOE_ROWS)
    return _combine(x2d, tok_slots, tok_w, ys, rows=256)


def _ff_in(w):
    return w.astype(BF16)


def _ff_out(w):
    return w.astype(BF16)


def kernel(x, norm_mix, w_in, hgrn_lower_bound, hgrn_out_norm, gmlp_v_norm, gmlp_w_s, gmlp_b_s,
           sb_q_norm, sb_k_norm, w_branch, w_out, norm_ffn, ffn_w_gate, ffn_w_up, ffn_w_down,
           router_w, router_b, moe_w_gate, moe_w_up, moe_w_down):
    batch, seq, d = x.shape
    depth = w_in.shape[0]
    n = batch * seq
    x2d = x.reshape(n, d)
    n_gate = N_BRANCH * d
    n_mix = w_in.shape[2] - n_gate
    gate_col0 = 0
    hg_col0 = n_gate // HG_W
    gm_col0 = hg_col0 + 4
    sb_col0 = gm_col0 + 2
    for l in range(depth):
        w_l = jnp.concatenate([w_in[l, :, n_mix:], w_in[l, :, :n_mix]], axis=1).astype(BF16)
        p = _inproj(x2d, norm_mix[l][None], w_l, tm=1024, tn=2560)
        oa = _hgrn(p, hgrn_lower_bound, hgrn_out_norm[l][None], l, hg_col0, batch, seq, rows=512)
        ob = _gmlp(p, gmlp_v_norm[l][None], gmlp_w_s[l], gmlp_b_s[l][:, :, None], gm_col0, rows=512)
        oc = _sb(p, jnp.tile(sb_q_norm[l], SB_HEADS)[None], jnp.tile(sb_k_norm[l], SB_HEADS)[None],
                 sb_col0, batch, seq)
        dense = l % 2 == 0
        outs = _merge(x2d, oa, ob, oc, p, gate_col0, w_branch[l].astype(BF16), w_out[l].astype(BF16),
                      norm_ffn[l][None], rows=512, emit_norm=dense)
        i = l // 2
        if dense:
            x2d = _ffn(outs[0], outs[1], _ff_in(ffn_w_gate[i]), _ff_in(ffn_w_up[i]), _ff_out(ffn_w_down[i]),
                       rows=512)
        else:
            x2d = _moe(outs[0], norm_ffn[l][None], router_w[i], router_b[i][None],
                       _ff_in(moe_w_gate[i]), _ff_in(moe_w_up[i]), _ff_out(moe_w_down[i]))
    return x2d.reshape(batch, seq, d)
```
